```python
import math
import jax, jax.numpy as jnp
from jax import lax
import numpy as np

D_MODEL = 1024
BATCH = 8
SEQ = 2048
DEPTH = 2

CHUNK = 64
Q_BLOCK = 128
EPS = 1e-6
GDN_HEADS = 4
GDN_DK = 128
GDN_DV = 128
GDN_QK_W = GDN_HEADS * GDN_DK
GDN_V_W = GDN_HEADS * GDN_DV
CONV_K = 4
MLA_HEADS = 8
MLA_NOPE = 64
MLA_ROPE = 32
MLA_V = 64
Q_LORA = 384
KV_LORA = 256
ROPE_THETA = 10000.0
N_BRANCH = 2
D_FF = 3584
N_EXPERTS = 8
TOP_K = 2
D_FF_EXPERT = 1792
N_DENSE = (DEPTH + 1) // 2
N_MOE = DEPTH // 2
SPLITS = [GDN_QK_W, GDN_QK_W, GDN_V_W, GDN_V_W, GDN_HEADS, GDN_HEADS,
          Q_LORA, KV_LORA, MLA_ROPE, N_BRANCH * D_MODEL]
IN_COLS = sum(SPLITS)

kernel_name = 'hybrid_gdn_mla_moe_streaming_block'

F32 = jnp.float32


def rmsnorm(x, w):
    xf = x.astype(F32)
    y = xf * lax.rsqrt(jnp.mean(xf * xf, axis=-1, keepdims=True) + EPS)
    return (y * w.astype(F32)).astype(x.dtype)


def l2norm(x):
    return x * lax.rsqrt(jnp.sum(x * x, axis=-1, keepdims=True) + EPS)


def rope(t, cos, sin):
    half = t.shape[-1] // 2
    t1, t2 = t[..., :half].astype(F32), t[..., half:].astype(F32)
    return jnp.concatenate([t1 * cos - t2 * sin, t2 * cos + t1 * sin], axis=-1).astype(t.dtype)


def causal_dwconv(x, w):
    c = x.shape[-1]
    return lax.conv_general_dilated(
        x, w[:, None, :].astype(x.dtype), window_strides=(1,), padding=[(CONV_K - 1, 0)],
        dimension_numbers=('NWC', 'WIO', 'NWC'), feature_group_count=c)


def gated_delta_chunked(q, k, v, g, beta):
    b, s, h, dk = q.shape
    dv = v.shape[-1]
    nc = s // CHUNK

    def chunks(t):
        return t.reshape(b, nc, CHUNK, h, -1).transpose(0, 1, 3, 2, 4)

    qc, kc, vc = chunks(q), chunks(k), chunks(v)
    gc = g.reshape(b, nc, CHUNK, h).transpose(0, 1, 3, 2)
    bc = beta.reshape(b, nc, CHUNK, h).transpose(0, 1, 3, 2)
    G = jnp.cumsum(gc, axis=-1)
    i = jnp.arange(CHUNK)
    incl = i[:, None] >= i[None, :]
    strict = i[:, None] > i[None, :]
    decay = jnp.exp(jnp.where(incl, G[..., :, None] - G[..., None, :], -jnp.inf))
    kb = kc * bc[..., None]
    lower = jnp.where(strict, jnp.einsum('bnhid,bnhjd->bnhij', kb, kc) * decay, 0.0)
    a_mat = lower + jnp.eye(CHUNK, dtype=F32)
    rhs = jnp.concatenate([vc * bc[..., None], kb * jnp.exp(G)[..., None]], axis=-1)
    sol = lax.linalg.triangular_solve(a_mat, rhs, left_side=True, lower=True, unit_diagonal=True)
    u, w = sol[..., :dv], sol[..., dv:]
    intra = jnp.einsum('bnhid,bnhjd->bnhij', qc, kc) * decay
    q_dec = qc * jnp.exp(G)[..., None]
    k_dec = kc * jnp.exp(G[..., -1:] - G)[..., None]
    chunk_decay = jnp.exp(G[..., -1])

    def step(state, inp):
        u_c, w_c, qd, kd, at, cd = inp
        v_new = u_c - jnp.einsum('bhcd,bhde->bhce', w_c, state)
        o = jnp.einsum('bhcd,bhde->bhce', qd, state) + jnp.einsum('bhij,bhje->bhie', at, v_new)
        state = state * cd[..., None, None] + jnp.einsum('bhcd,bhce->bhde', kd, v_new)
        return state, o

    xs = (jnp.moveaxis(u, 1, 0), jnp.moveaxis(w, 1, 0), jnp.moveaxis(q_dec, 1, 0),
          jnp.moveaxis(k_dec, 1, 0), jnp.moveaxis(intra, 1, 0), jnp.moveaxis(chunk_decay, 1, 0))
    s0 = jnp.zeros((b, h, dk, dv), F32)
    _, o = lax.scan(step, s0, xs)
    return o.transpose(1, 0, 3, 2, 4).reshape(b, s, h, dv)


def gdn_branch(q, k, v, z, a, bt, conv_w, a_log, dt_bias, out_norm):
    b, s, _ = q.shape
    qkv = jax.nn.silu(causal_dwconv(jnp.concatenate([q, k, v], axis=-1), conv_w))
    q = qkv[..., :GDN_QK_W].reshape(b, s, GDN_HEADS, GDN_DK).astype(F32)
    k = qkv[..., GDN_QK_W:2 * GDN_QK_W].reshape(b, s, GDN_HEADS, GDN_DK).astype(F32)
    v = qkv[..., 2 * GDN_QK_W:].reshape(b, s, GDN_HEADS, GDN_DV).astype(F32)
    q = l2norm(q) * (GDN_DK ** -0.5)
    k = l2norm(k)
    g = -jnp.exp(a_log.astype(F32)) * jax.nn.softplus(a.astype(F32) + dt_bias.astype(F32))
    beta = jax.nn.sigmoid(bt.astype(F32))
    o = gated_delta_chunked(q, k, v, g, beta)
    o = o * lax.rsqrt(jnp.mean(o * o, axis=-1, keepdims=True) + EPS) * out_norm.astype(F32)
    o = o * jax.nn.silu(z.reshape(b, s, GDN_HEADS, GDN_DV).astype(F32))
    return o.reshape(b, s, GDN_V_W).astype(z.dtype)


def chunk_causal_attention(q, k, v):
    b, s, h, dq = q.shape
    nb = s // Q_BLOCK
    scale = dq ** -0.5
    qb = q.reshape(b, nb, Q_BLOCK, h, dq).transpose(1, 0, 2, 3, 4)
    key_chunk = jnp.arange(s) // CHUNK

    def one_block(args):
        qi, bi = args
        sc = jnp.einsum('bqhd,bkhd->bhqk', qi, k).astype(F32) * scale
        q_chunk = (bi * Q_BLOCK + jnp.arange(Q_BLOCK)) // CHUNK
        mask = key_chunk[None, :] <= q_chunk[:, None]
        p = jax.nn.softmax(jnp.where(mask, sc, -jnp.inf), axis=-1)
        return jnp.einsum('bhqk,bkhe->bqhe', p.astype(v.dtype), v)

    o = lax.map(one_block, (qb, jnp.arange(nb)))
    return o.transpose(1, 0, 2, 3, 4).reshape(b, s, h, v.shape[-1])


def mla_branch(c_q, c_kv, k_r, cos, sin, q_a_norm, w_q_b, kv_a_norm, w_kv_b):
    b, s, _ = c_q.shape
    q = (rmsnorm(c_q, q_a_norm) @ w_q_b).reshape(b, s, MLA_HEADS, MLA_NOPE + MLA_ROPE)
    q_pe = rope(q[..., MLA_NOPE:], cos[:, :, None], sin[:, :, None])
    q = jnp.concatenate([q[..., :MLA_NOPE], q_pe], axis=-1)
    kv = (rmsnorm(c_kv, kv_a_norm) @ w_kv_b).reshape(b, s, MLA_HEADS, MLA_NOPE + MLA_V)
    k_pe = rope(k_r, cos, sin)
    k = jnp.concatenate(
        [kv[..., :MLA_NOPE], jnp.broadcast_to(k_pe[:, :, None, :], (b, s, MLA_HEADS, MLA_ROPE))], axis=-1)
    v = kv[..., MLA_NOPE:]
    o = chunk_causal_attention(q, k, v)
    return o.reshape(b, s, MLA_HEADS * MLA_V)


def swiglu(t, w1, w3, w2):
    return (jax.nn.silu(t @ w1) * (t @ w3)) @ w2


def moe_ffn(h, router_w, w1, w3, w2):
    b, s, d = h.shape
    t = h.reshape(-1, d)
    logits = (t @ router_w).astype(F32)
    top_logits, top_idx = lax.top_k(logits, TOP_K)
    top_w = jax.nn.softmax(top_logits, axis=-1)
    combine = jnp.sum(jax.nn.one_hot(top_idx, N_EXPERTS, dtype=F32) * top_w[..., None], axis=1)
    out = jnp.zeros_like(t)
    for e in range(N_EXPERTS):
        out = out + combine[:, e:e + 1].astype(t.dtype) * swiglu(t, w1[e], w3[e], w2[e])
    return out.reshape(b, s, d)


def setup_inputs(seed: int = 0) -> dict:
    key = jax.random.key(seed)
    ks = jax.random.split(key, 32)
    L = DEPTH

    def nrm(k, shape, fan_in):
        return jax.random.normal(k, shape, F32) * (fan_in ** -0.5)

    def gain(k, shape):
        return 1.0 + 0.02 * jax.random.normal(k, shape, F32)

    x = jax.random.normal(ks[0], (BATCH, SEQ, D_MODEL), F32)
    gaps = jax.random.randint(ks[1], (BATCH, SEQ), 1, 4, dtype=jnp.int32)
    positions = (jnp.cumsum(gaps, axis=1) - gaps[:, :1]).astype(jnp.int32)
    dt = jnp.exp(jax.random.uniform(ks[7], (L, GDN_HEADS), F32, math.log(1e-3), math.log(1e-1)))
    return {
        'x': x,
        'positions': positions,
        'mix_norm': gain(ks[2], (L, D_MODEL)),
        'w_in': nrm(ks[3], (L, D_MODEL, IN_COLS), D_MODEL),
        'gate_bias': 0.01 * jax.random.normal(ks[4], (L, N_BRANCH * D_MODEL), F32),
        'conv_w': nrm(ks[5], (L, CONV_K, 2 * GDN_QK_W + GDN_V_W), CONV_K),
        'a_log': jnp.log(jax.random.uniform(ks[6], (L, GDN_HEADS), F32, 1.0, 16.0)),
        'dt_bias': dt + jnp.log(-jnp.expm1(-dt)),
        'gdn_out_norm': gain(ks[8], (L, GDN_DV)),
        'w_gdn_o': nrm(ks[9], (L, GDN_V_W, D_MODEL), GDN_V_W),
        'q_a_norm': gain(ks[10], (L, Q_LORA)),
        'w_q_b': nrm(ks[11], (L, Q_LORA, MLA_HEADS * (MLA_NOPE + MLA_ROPE)), Q_LORA),
        'kv_a_norm': gain(ks[12], (L, KV_LORA)),
        'w_kv_b': nrm(ks[13], (L, KV_LORA, MLA_HEADS * (MLA_NOPE + MLA_V)), KV_LORA),
        'w_mla_o': nrm(ks[14], (L, MLA_HEADS * MLA_V, D_MODEL), MLA_HEADS * MLA_V),
        'w_out': nrm(ks[15], (L, D_MODEL, D_MODEL), D_MODEL),
        'ffn_norm': gain(ks[16], (L, D_MODEL)),
        'dense_w1': nrm(ks[17], (N_DENSE, D_MODEL, D_FF), D_MODEL),
        'dense_w3': nrm(ks[18], (N_DENSE, D_MODEL, D_FF), D_MODEL),
        'dense_w2': nrm(ks[19], (N_DENSE, D_FF, D_MODEL), D_FF),
        'router_w': nrm(ks[20], (N_MOE, D_MODEL, N_EXPERTS), D_MODEL),
        'moe_w1': nrm(ks[21], (N_MOE, N_EXPERTS, D_MODEL, D_FF_EXPERT), D_MODEL),
        'moe_w3': nrm(ks[22], (N_MOE, N_EXPERTS, D_MODEL, D_FF_EXPERT), D_MODEL),
        'moe_w2': nrm(ks[23], (N_MOE, N_EXPERTS, D_FF_EXPERT, D_MODEL), D_FF_EXPERT),
        'final_norm': gain(ks[24], (D_MODEL,)),
    }


def reference(x, positions, mix_norm, w_in, gate_bias, conv_w, a_log, dt_bias, gdn_out_norm,
              w_gdn_o, q_a_norm, w_q_b, kv_a_norm, w_kv_b, w_mla_o, w_out, ffn_norm,
              dense_w1, dense_w3, dense_w2, router_w, moe_w1, moe_w3, moe_w2, final_norm):
    split_idx = np.cumsum(SPLITS)[:-1].tolist()
    inv_freq = 1.0 / (ROPE_THETA ** (jnp.arange(0, MLA_ROPE, 2, dtype=F32) / MLA_ROPE))
    ang = positions.astype(F32)[..., None] * inv_freq
    cos, sin = jnp.cos(ang), jnp.sin(ang)
    for l in range(DEPTH):
        h = rmsnorm(x, mix_norm[l])
        proj = h @ w_in[l]
        q_g, k_g, v_g, z_g, a_g, b_g, c_q, c_kv, k_r, gate = jnp.split(proj, split_idx, axis=-1)
        o_a = gdn_branch(q_g, k_g, v_g, z_g, a_g, b_g, conv_w[l], a_log[l], dt_bias[l],
                         gdn_out_norm[l]) @ w_gdn_o[l]
        o_b = mla_branch(c_q, c_kv, k_r, cos, sin, q_a_norm[l], w_q_b[l], kv_a_norm[l],
                         w_kv_b[l]) @ w_mla_o[l]
        g = jax.nn.sigmoid((gate + gate_bias[l]).astype(F32)).astype(x.dtype)
        merged = g[..., :D_MODEL] * o_a + g[..., D_MODEL:] * o_b
        x = x + merged @ w_out[l]
        h2 = rmsnorm(x, ffn_norm[l])
        if l % 2 == 0:
            x = x + swiglu(h2, dense_w1[l // 2], dense_w3[l // 2], dense_w2[l // 2])
        else:
            x = x + moe_ffn(h2, router_w[l // 2], moe_w1[l // 2], moe_w3[l // 2], moe_w2[l // 2])
    return rmsnorm(x, final_norm)
```

```python
import functools
import math

import jax
import jax.numpy as jnp
from jax import lax
from jax.experimental import pallas as pl
from jax.experimental.pallas import tpu as pltpu

F32 = jnp.float32
BF16 = jnp.bfloat16
HIGHEST = lax.Precision.HIGHEST

LANES = 128
EPS = 1e-6
CHUNK = 64
GDN_HEADS = 4
GDN_D = 128
GDN_BLOCK = 128
CONV_K = 4
MLA_HEADS = 8
MLA_NOPE = 64
MLA_ROPE = 32
MLA_V = 64
Q_LORA = 384
KV_LORA = 256
ROPE_THETA = 10000.0
N_EXPERTS = 8

COL_GATE = 0
COL_CQ = 16
COL_SMALL = 19
COL_CKV = 20
COL_Q = 22
COL_K = 26
COL_V = 30
COL_Z = 34
IN_COLS_PAD = 38 * LANES
ROPE_LANE0 = 64


def _dot(a, b, precision=None):
    return jnp.dot(a, b, preferred_element_type=F32, precision=precision)


def _dot_nt(a, b, precision=None):
    return lax.dot_general(a, b, (((1,), (1,)), ((), ())), preferred_element_type=F32,
                           precision=precision)


def _dot_tn(a, b, precision=None):
    return lax.dot_general(a, b, (((0,), (0,)), ((), ())), preferred_element_type=F32,
                           precision=precision)


def _sigmoid(x):
    return 1.0 / (1.0 + jnp.exp(-x))


def _silu(x):
    return x * _sigmoid(x)


def _rms(x, gain):
    return x * lax.rsqrt(jnp.mean(x * x, axis=-1, keepdims=True) + EPS) * gain


def _inproj_kernel(x_ref, g_ref, w_ref, o_ref):
    h = _rms(x_ref[...], g_ref[...]).astype(BF16)
    o_ref[...] = _dot(h, w_ref[...])


def _inproj(x, gain, w, *, tm, tn):
    n, d = x.shape
    m = w.shape[1]
    return pl.pallas_call(
        _inproj_kernel,
        grid=(m // tn, n // tm),
        in_specs=[pl.BlockSpec((tm, d), lambda j, i: (i, 0)),
                  pl.BlockSpec((1, d), lambda j, i: (0, 0)),
                  pl.BlockSpec((d, tn), lambda j, i: (0, j))],
        out_specs=pl.BlockSpec((tm, tn), lambda j, i: (i, j)),
        out_shape=jax.ShapeDtypeStruct((n, m), F32),
        compiler_params=pltpu.CompilerParams(dimension_semantics=("arbitrary", "arbitrary")),
        name="inproj",
    )(x, gain, w)


def _gdn_kernel(q_ref, k_ref, v_ref, z_ref, ab_ref, cwq_ref, cwk_ref, cwv_ref, alog_ref, dtb_ref,
                onorm_ref, o_ref, u_s, w_s, qd_s, kd_s, at_s, cd_s, *, seq):
    blk = GDN_BLOCK
    head = pl.program_id(1)
    nblk = seq // blk
    row = lax.broadcasted_iota(jnp.int32, (blk, blk), 0)
    col = lax.broadcasted_iota(jnp.int32, (blk, blk), 1)
    sel_a = (row == head).astype(F32)
    sel_b = (row == head + GDN_HEADS).astype(F32)
    incl = row >= col
    strict = row > col
    eye = row == col
    tril_ones = incl.astype(F32)
    ones = jnp.ones((blk, blk), F32)
    neg_a = -jnp.exp(alog_ref[...])
    dtb = dtb_ref[...]

    def conv_silu(x_ref, cw_ref, r0, first):
        cur = x_ref[pl.ds(r0, blk), :]
        p0 = pl.multiple_of(jnp.maximum(r0 - 8, 0), 8)
        prev = x_ref[pl.ds(p0, 8), :] * jnp.where(first, 0.0, 1.0)
        ext = jnp.concatenate([prev, cur], axis=0)
        cw = cw_ref[...]
        y = cur * cw[CONV_K - 1:CONV_K, :]
        for s in range(1, CONV_K):
            y = y + pltpu.roll(ext, s, 0)[8:, :] * cw[CONV_K - 1 - s:CONV_K - s, :]
        return _silu(y)

    def l2n(x):
        return x * lax.rsqrt(jnp.sum(x * x, axis=-1, keepdims=True) + EPS)

    def solve_phase(c, carry):
        r0 = pl.multiple_of(c * blk, blk)
        first = c == 0
        q = l2n(conv_silu(q_ref, cwq_ref, r0, first)) * (GDN_D ** -0.5)
        k = l2n(conv_silu(k_ref, cwk_ref, r0, first))
        v = conv_silu(v_ref, cwv_ref, r0, first)
        ab = ab_ref[pl.ds(r0, blk), :]
        a_b = _dot(ab, sel_a, HIGHEST)
        b_b = _dot(ab, sel_b, HIGHEST)
        xa = a_b + dtb
        softplus = jnp.maximum(xa, 0.0) + jnp.log(1.0 + jnp.exp(-jnp.abs(xa)))
        g = neg_a * softplus
        beta = _sigmoid(b_b)
        gc = _dot(tril_ones, g, HIGHEST)
        gr = _dot(ones, jnp.where(eye, gc, 0.0), HIGHEST)
        decay = jnp.exp(jnp.where(incl, gc - gr, -jnp.inf))
        eg = jnp.exp(gc)
        kb = k * beta
        lower = jnp.where(strict, _dot_nt(kb, k, HIGHEST) * decay, 0.0)
        inv = jnp.where(eye, 1.0, 0.0) - jnp.where((row >> 1) == (col >> 1), lower, 0.0)
        s = 2
        while s < blk:
            sh = s.bit_length() - 1
            off = jnp.where(((row >> (sh + 1)) == (col >> (sh + 1))) & (((row >> sh) & 1) == 1)
                            & (((col >> sh) & 1) == 0), lower, 0.0)
            inv = inv - _dot(_dot(inv, off, HIGHEST), inv, HIGHEST)
            s *= 2
        u = _dot(inv, v * beta, HIGHEST)
        w = _dot(inv, kb * eg, HIGHEST)
        g_last = gc[blk - 1:blk, :]
        u_s[pl.ds(r0, blk), :] = u
        w_s[pl.ds(r0, blk), :] = w
        qd_s[pl.ds(r0, blk), :] = q * eg
        kd_s[pl.ds(r0, blk), :] = k * jnp.exp(g_last - gc)
        at_s[pl.ds(r0, blk), :] = _dot_nt(q, k, HIGHEST) * decay
        cd_s[pl.ds(c, 1), :] = jnp.exp(g_last)
        return carry

    lax.fori_loop(0, nblk, solve_phase, 0)

    onorm = onorm_ref[...]

    def scan_phase(c, state):
        r0 = pl.multiple_of(c * blk, blk)
        v_new = u_s[pl.ds(r0, blk), :] - _dot(w_s[pl.ds(r0, blk), :], state, HIGHEST)
        o = _dot(qd_s[pl.ds(r0, blk), :], state, HIGHEST) + _dot(at_s[pl.ds(r0, blk), :], v_new, HIGHEST)
        state = state * cd_s[pl.ds(c, 1), :] + _dot_tn(kd_s[pl.ds(r0, blk), :], v_new, HIGHEST)
        o = _rms(o, onorm) * _silu(z_ref[pl.ds(r0, blk), :])
        o_ref[pl.ds(r0, blk), :] = o.astype(o_ref.dtype)
        return state

    lax.fori_loop(0, nblk, scan_phase, jnp.zeros((GDN_D, GDN_D), F32))


def _gdn(proj3, conv_w, a_log_b, dt_bias_b, out_norm):
    b, s, _ = proj3.shape

    def col(c0):
        return pl.BlockSpec((None, s, LANES), lambda i, h: (i, 0, c0 + h))

    def cw(c0):
        return pl.BlockSpec((CONV_K, LANES), lambda i, h: (0, c0 + h))

    head_row = pl.BlockSpec((None, 1, LANES), lambda i, h: (h, 0, 0))
    seq_buf = pltpu.VMEM((s, LANES), F32)
    return pl.pallas_call(
        functools.partial(_gdn_kernel, seq=s),
        grid=(b, GDN_HEADS),
        in_specs=[col(COL_Q), col(COL_K), col(COL_V), col(COL_Z),
                  pl.BlockSpec((None, s, LANES), lambda i, h: (i, 0, COL_SMALL)),
                  cw(0), cw(GDN_HEADS), cw(2 * GDN_HEADS),
                  head_row, head_row,
                  pl.BlockSpec((1, LANES), lambda i, h: (0, 0))],
        out_specs=pl.BlockSpec((None, s, LANES), lambda i, h: (i, 0, h)),
        out_shape=jax.ShapeDtypeStruct((b, s, GDN_HEADS * GDN_D), BF16),
        scratch_shapes=[seq_buf, seq_buf, seq_buf, seq_buf, seq_buf,
                        pltpu.VMEM((s // GDN_BLOCK, LANES), F32)],
        compiler_params=pltpu.CompilerParams(dimension_semantics=("arbitrary", "arbitrary")),
        name="gdn",
    )(proj3, proj3, proj3, proj3, proj3, conv_w, conv_w, conv_w, a_log_b, dt_bias_b, out_norm)


def _mla_prep_kernel(cqa_ref, ckv_ref, pos_ref, invf_ref, qn_ref, kvn_ref, wq_ref, wk_ref, wv_ref,
                     q_ref, k_ref, v_ref):
    cqa = cqa_ref[...]
    c_q = cqa[:, :Q_LORA]
    small = cqa[:, Q_LORA:]
    lane = lax.broadcasted_iota(jnp.int32, small.shape, 1)
    ang = pos_ref[...] * invf_ref[...]
    cosv = jnp.cos(ang)
    sinv = jnp.sin(ang)
    lo = (lane >= ROPE_LANE0) & (lane < ROPE_LANE0 + MLA_ROPE // 2)
    hi = (lane >= ROPE_LANE0 + MLA_ROPE // 2) & (lane < ROPE_LANE0 + MLA_ROPE)
    sin_lo = jnp.where(lo, -sinv, 0.0)
    sin_hi = jnp.where(hi, sinv, 0.0)

    def rotate(t, cos_t):
        return (t * cos_t + pltpu.roll(t, LANES - MLA_ROPE // 2, 1) * sin_lo
                + pltpu.roll(t, MLA_ROPE // 2, 1) * sin_hi)

    scale = (MLA_NOPE + MLA_ROPE) ** -0.5
    q = _dot(_rms(c_q, qn_ref[...]).astype(BF16), wq_ref[...])
    cos_q = jnp.where(lo | hi, cosv, 1.0) * scale
    sin_lo_q, sin_hi_q = sin_lo, sin_hi
    hkv = _rms(ckv_ref[...], kvn_ref[...]).astype(BF16)
    k_nope = _dot(hkv, wk_ref[...])
    k_pe = rotate(small, jnp.where(lo | hi, cosv, 0.0))
    for h in range(MLA_HEADS):
        sl = slice(h * LANES, (h + 1) * LANES)
        qh = q[:, sl]
        qh = (qh * cos_q + (pltpu.roll(qh, LANES - MLA_ROPE // 2, 1) * sin_lo_q
                            + pltpu.roll(qh, MLA_ROPE // 2, 1) * sin_hi_q) * scale)
        q_ref[:, sl] = qh.astype(q_ref.dtype)
        k_ref[:, sl] = (k_nope[:, sl] + k_pe).astype(k_ref.dtype)
    v_ref[...] = _dot(hkv, wv_ref[...]).astype(v_ref.dtype)


def _mla_prep(proj, pos, invf, q_a_norm, kv_a_norm, wq, wk, wv, *, tm):
    n = proj.shape[0]
    hq = MLA_HEADS * LANES
    hv = MLA_HEADS * MLA_V
    full = lambda shape: pl.BlockSpec(shape, lambda i: (0, 0))
    return pl.pallas_call(
        _mla_prep_kernel,
        grid=(n // tm,),
        in_specs=[pl.BlockSpec((tm, Q_LORA + LANES), lambda i: (i, COL_CQ * LANES // (Q_LORA + LANES))),
                  pl.BlockSpec((tm, KV_LORA), lambda i: (i, COL_CKV * LANES // KV_LORA)),
                  pl.BlockSpec((tm, 1), lambda i: (i, 0)),
                  full((1, LANES)), full((1, Q_LORA)), full((1, KV_LORA)),
                  full((Q_LORA, hq)), full((KV_LORA, hq)), full((KV_LORA, hv))],
        out_specs=[pl.BlockSpec((tm, hq), lambda i: (i, 0)),
                   pl.BlockSpec((tm, hq), lambda i: (i, 0)),
                   pl.BlockSpec((tm, hv), lambda i: (i, 0))],
        out_shape=[jax.ShapeDtypeStruct((n, hq), BF16),
                   jax.ShapeDtypeStruct((n, hq), BF16),
                   jax.ShapeDtypeStruct((n, hv), BF16)],
        compiler_params=pltpu.CompilerParams(dimension_semantics=("arbitrary",)),
        name="mla_prep",
    )(proj, proj, pos, invf, q_a_norm, kv_a_norm, wq, wk, wv)


def _attn_kernel(q_ref, k_ref, v_ref, o_ref, *, tq):
    qi = pl.program_id(2)
    r_chunk = lax.broadcasted_iota(jnp.int32, (tq, tq), 0) // CHUNK
    c_chunk = lax.broadcasted_iota(jnp.int32, (tq, tq), 1) // CHUNK
    visible = c_chunk <= r_chunk
    outs = []
    for hh in range(2):
        hs = slice(hh * LANES, (hh + 1) * LANES)
        q = q_ref[:, hs]

        def step(kb, carry, diagonal):
            m, l, acc = carry
            k0 = pl.multiple_of(kb * tq, tq)
            s = _dot_nt(q, k_ref[pl.ds(k0, tq), hs])
            if diagonal:
                s = jnp.where(visible, s, -jnp.inf)
            m_new = jnp.maximum(m, jnp.max(s, axis=-1, keepdims=True))
            alpha = jnp.exp(m - m_new)
            p = jnp.exp(s - m_new)
            l = alpha * l + jnp.sum(p, axis=-1, keepdims=True)
            acc = alpha * acc + _dot(p.astype(BF16), v_ref[pl.ds(k0, tq), :])
            return m_new, l, acc

        init = (jnp.full((tq, 1), -jnp.inf, F32), jnp.zeros((tq, 1), F32), jnp.zeros((tq, LANES), F32))
        carry = lax.fori_loop(0, qi, lambda kb, c: step(kb, c, False), init)
        _, l, acc = step(qi, carry, True)
        outs.append(acc / l)
    lane = lax.broadcasted_iota(jnp.int32, (tq, LANES), 1)
    o_ref[...] = jnp.where(lane < MLA_V, outs[0], outs[1]).astype(o_ref.dtype)


def _attention(q3, k3, v3, *, tq):
    b, s, _ = q3.shape
    return pl.pallas_call(
        functools.partial(_attn_kernel, tq=tq),
        grid=(b, MLA_HEADS // 2, s // tq),
        in_specs=[pl.BlockSpec((None, tq, 2 * LANES), lambda i, p, j: (i, j, p)),
                  pl.BlockSpec((None, s, 2 * LANES), lambda i, p, j: (i, 0, p)),
                  pl.BlockSpec((None, s, 2 * MLA_V), lambda i, p, j: (i, 0, p))],
        out_specs=pl.BlockSpec((None, tq, 2 * MLA_V), lambda i, p, j: (i, j, p)),
        out_shape=jax.ShapeDtypeStruct((b, s, MLA_HEADS * MLA_V), BF16),
        compiler_params=pltpu.CompilerParams(dimension_semantics=("arbitrary",) * 3),
        name="attention",
    )(q3, k3, v3)


def _merge_kernel(x_ref, oa_ref, ob_ref, gate_ref, gb_ref, wa_ref, wb_ref, wo_ref, o_ref):
    d = x_ref.shape[1]
    g = _sigmoid(gate_ref[...] + gb_ref[...])
    merged = g[:, :d] * _dot(oa_ref[...], wa_ref[...]) + g[:, d:] * _dot(ob_ref[...], wb_ref[...])
    o_ref[...] = x_ref[...] + _dot(merged.astype(BF16), wo_ref[...])


def _merge(x, o_a, o_b, proj, gate_bias, w_a, w_b, w_o, *, tm):
    n, d = x.shape
    full = lambda a: pl.BlockSpec(a.shape, lambda i: (0, 0))
    return pl.pallas_call(
        _merge_kernel,
        grid=(n // tm,),
        in_specs=[pl.BlockSpec((tm, d), lambda i: (i, 0)),
                  pl.BlockSpec((tm, o_a.shape[1]), lambda i: (i, 0)),
                  pl.BlockSpec((tm, o_b.shape[1]), lambda i: (i, 0)),
                  pl.BlockSpec((tm, 2 * d), lambda i: (i, COL_GATE)),
                  full(gate_bias), full(w_a), full(w_b), full(w_o)],
        out_specs=pl.BlockSpec((tm, d), lambda i: (i, 0)),
        out_shape=jax.ShapeDtypeStruct((n, d), F32),
        compiler_params=pltpu.CompilerParams(dimension_semantics=("arbitrary",)),
        name="merge",
    )(x, o_a, o_b, proj, gate_bias, w_a, w_b, w_o)


def _ffn_kernel(*refs, routed, final_norm):
    refs = list(refs)
    x_ref, g_ref = refs[:2]
    refs = refs[2:]
    if routed:
        rw_ref = refs.pop(0)
    w1_ref, w3_ref, w2_ref = refs[:3]
    refs = refs[3:]
    if final_norm:
        fin_ref = refs.pop(0)
    o_ref, h_s, acc_s = refs[:3]
    if routed:
        comb_s = refs[3]
    e = pl.program_id(1)
    f = pl.program_id(2)
    lane = lax.broadcasted_iota(jnp.int32, (x_ref.shape[0], LANES), 1)

    @pl.when((e == 0) & (f == 0))
    def _():
        x = x_ref[...]
        h = _rms(x, g_ref[...])
        h_s[...] = h.astype(BF16)
        acc_s[...] = x
        if routed:
            lane_f = lane.astype(F32)
            logits = jnp.where(lane < N_EXPERTS, _dot(h, rw_ref[...], HIGHEST), -jnp.inf)
            m1 = jnp.max(logits, axis=-1, keepdims=True)
            i1 = jnp.min(jnp.where(logits == m1, lane_f, float(LANES)), axis=-1, keepdims=True)
            rest = jnp.where(lane_f == i1, -jnp.inf, logits)
            m2 = jnp.max(rest, axis=-1, keepdims=True)
            i2 = jnp.min(jnp.where(rest == m2, lane_f, float(LANES)), axis=-1, keepdims=True)
            t = jnp.exp(m2 - m1)
            comb_s[...] = (jnp.where(lane_f == i1, 1.0 / (1.0 + t), 0.0)
                           + jnp.where(lane_f == i2, t / (1.0 + t), 0.0))

    h = h_s[...]
    hid = _silu(_dot(h, w1_ref[...])) * _dot(h, w3_ref[...])
    y = _dot(hid.astype(BF16), w2_ref[...])
    if routed:
        y = y * jnp.sum(jnp.where(lane == e, comb_s[...], 0.0), axis=-1, keepdims=True)
    acc_s[...] += y

    @pl.when((e == pl.num_programs(1) - 1) & (f == pl.num_programs(2) - 1))
    def _():
        out = acc_s[...]
        if final_norm:
            out = _rms(out, fin_ref[...])
        o_ref[...] = out


def _ffn(x, gain, w1, w3, w2, *, tm, tf, router_w=None, final_gain=None):
    n, d = x.shape
    ne, _, dff = w1.shape
    routed = router_w is not None
    final_norm = final_gain is not None
    row = lambda shape: pl.BlockSpec(shape, lambda i, e, f: (0, 0))
    in_specs = [pl.BlockSpec((tm, d), lambda i, e, f: (i, 0)), row((1, d))]
    args = [x, gain]
    if routed:
        in_specs.append(row(router_w.shape))
        args.append(router_w)
    in_specs += [pl.BlockSpec((None, d, tf), lambda i, e, f: (e, 0, f)),
                 pl.BlockSpec((None, d, tf), lambda i, e, f: (e, 0, f)),
                 pl.BlockSpec((None, tf, d), lambda i, e, f: (e, f, 0))]
    args += [w1, w3, w2]
    if final_norm:
        in_specs.append(row((1, d)))
        args.append(final_gain)
    scratch = [pltpu.VMEM((tm, d), BF16), pltpu.VMEM((tm, d), F32)]
    if routed:
        scratch.append(pltpu.VMEM((tm, LANES), F32))
    return pl.pallas_call(
        functools.partial(_ffn_kernel, routed=routed, final_norm=final_norm),
        grid=(n // tm, ne, dff // tf),
        in_specs=in_specs,
        out_specs=pl.BlockSpec((tm, d), lambda i, e, f: (i, 0)),
        out_shape=jax.ShapeDtypeStruct((n, d), F32),
        scratch_shapes=scratch,
        compiler_params=pltpu.CompilerParams(dimension_semantics=("arbitrary",) * 3),
        name="moe" if routed else "ffn",
    )(*args)


def _pad_cols(w, width):
    return jnp.pad(w, ((0, 0), (0, width - w.shape[1])))


def _layout_w_in(w):
    qk = GDN_HEADS * GDN_D
    o = 0
    parts = {}
    for name, width in (("q", qk), ("k", qk), ("v", qk), ("z", qk), ("a", GDN_HEADS), ("b", GDN_HEADS),
                        ("cq", Q_LORA), ("ckv", KV_LORA), ("kr", MLA_ROPE), ("gate", 2 * w.shape[0])):
        parts[name] = w[:, o:o + width]
        o += width
    small = jnp.concatenate(
        [_pad_cols(jnp.concatenate([parts["a"], parts["b"]], axis=1), ROPE_LANE0),
         _pad_cols(parts["kr"], LANES - ROPE_LANE0)], axis=1)
    return jnp.concatenate([parts["gate"], parts["cq"], small, parts["ckv"], parts["q"], parts["k"],
                            parts["v"], parts["z"]], axis=1).astype(BF16)


def _layout_heads(w, widths, pick, pad_to):
    per = sum(widths)
    k = w.shape[0]
    w = w.reshape(k, MLA_HEADS, per)
    start = sum(widths[:pick[0]])
    stop = sum(widths[:pick[1]])
    seg = w[:, :, start:stop]
    seg = jnp.pad(seg, ((0, 0), (0, 0), (0, pad_to - (stop - start))))
    return seg.reshape(k, MLA_HEADS * pad_to).astype(BF16)


def kernel(x, positions, mix_norm, w_in, gate_bias, conv_w, a_log, dt_bias, gdn_out_norm, w_gdn_o,
           q_a_norm, w_q_b, kv_a_norm, w_kv_b, w_mla_o, w_out, ffn_norm, dense_w1, dense_w3, dense_w2,
           router_w, moe_w1, moe_w3, moe_w2, final_norm):
    b, s, d = x.shape
    n = b * s
    depth = w_in.shape[0]
    xf = x.reshape(n, d)
    pos = positions.astype(F32).reshape(n, 1)
    inv_freq = 1.0 / (ROPE_THETA ** (jnp.arange(0, MLA_ROPE, 2, dtype=F32) / MLA_ROPE))
    invf = jnp.zeros((1, LANES), F32).at[0, ROPE_LANE0:ROPE_LANE0 + MLA_ROPE].set(
        jnp.concatenate([inv_freq, inv_freq]))
    row = lambda v: v.reshape(1, -1)
    head_lanes = lambda v: jnp.broadcast_to(v[:, None, None], (GDN_HEADS, 1, LANES))

    for l in range(depth):
        proj = _inproj(xf, row(mix_norm[l]), _layout_w_in(w_in[l]), tm=512, tn=IN_COLS_PAD // 2)
        o_a = _gdn(proj.reshape(b, s, IN_COLS_PAD), conv_w[l], head_lanes(a_log[l]),
                   head_lanes(dt_bias[l]), row(gdn_out_norm[l]))
        wq = _layout_heads(w_q_b[l], (MLA_NOPE, MLA_ROPE), (0, 2), LANES)
        wk = _layout_heads(w_kv_b[l], (MLA_NOPE, MLA_V), (0, 1), LANES)
        wv = _layout_heads(w_kv_b[l], (MLA_NOPE, MLA_V), (1, 2), MLA_V)
        q, k, v = _mla_prep(proj, pos, invf, row(q_a_norm[l]), row(kv_a_norm[l]), wq, wk, wv, tm=512)
        o_b = _attention(q.reshape(b, s, -1), k.reshape(b, s, -1), v.reshape(b, s, -1), tq=256)
        xf = _merge(xf, o_a.reshape(n, -1), o_b.reshape(n, -1), proj, row(gate_bias[l]),
                    w_gdn_o[l].astype(BF16), w_mla_o[l].astype(BF16), w_out[l].astype(BF16), tm=512)
        fin = row(final_norm) if l == depth - 1 else None
        if l % 2 == 0:
            j = l // 2
            xf = _ffn(xf, row(ffn_norm[l]), dense_w1[j:j + 1].astype(BF16), dense_w3[j:j + 1].astype(BF16),
                      dense_w2[j:j + 1].astype(BF16), tm=512, tf=512, final_gain=fin)
        else:
            j = l // 2
            xf = _ffn(xf, row(ffn_norm[l]), moe_w1[j].astype(BF16), moe_w3[j].astype(BF16),
                      moe_w2[j].astype(BF16), tm=512, tf=896,
                      router_w=_pad_cols(router_w[j], LANES), final_gain=fin)
    return xf.reshape(b, s, d)
```

```python
import functools
import math

import jax
import jax.numpy as jnp
from jax import lax
from jax.experimental import pallas as pl
from jax.experimental.pallas import tpu as pltpu

F32 = jnp.float32
BF16 = jnp.bfloat16
HIGHEST = lax.Precision.HIGHEST

LANES = 128
SUBLANES = 8
EPS = 1e-6
LOG2_E = math.log2(math.e)
CHUNK = 64
GDN_HEADS = 4
GDN_D = 128
GDN_BLOCK = 128
GDN_TILE = 512
CONV_K = 4
MLA_HEADS = 8
MLA_NOPE = 64
MLA_ROPE = 32
MLA_V = 64
Q_LORA = 384
KV_LORA = 256
ROPE_THETA = 10000.0
N_EXPERTS = 8

COL_GATE = 0
COL_Q = 16
COL_K = 20
COL_V = 24
COL_Z = 28
COL_CQ = 32
COL_SMALL = 35
COL_CKV = 36
IN_COLS_PAD = 38 * LANES
ROPE_LANE0 = 64


def _dot(a, b, precision=None):
    return jnp.dot(a, b, preferred_element_type=F32, precision=precision)


def _dot_nt(a, b, precision=None):
    return lax.dot_general(a, b, (((1,), (1,)), ((), ())), preferred_element_type=F32,
                           precision=precision)


def _sigmoid(x):
    return 1.0 / (1.0 + jnp.exp(-x))


def _silu(x):
    return x * _sigmoid(x)


def _rms(x, gain):
    return x * lax.rsqrt(jnp.mean(x * x, axis=-1, keepdims=True) + EPS) * gain


def _mm(a, b, nt=False):
    dot = _dot_nt if nt else _dot
    return dot(a.astype(BF16), b.astype(BF16))


def _inproj_kernel(x_ref, g_ref, w_ref, o_ref):
    h = _rms(x_ref[...], g_ref[...]).astype(BF16)
    o_ref[...] = _dot(h, w_ref[...])


def _inproj(x, gain, w, *, tm, tn):
    n, d = x.shape
    m = w.shape[1]
    return pl.pallas_call(
        _inproj_kernel,
        grid=(m // tn, n // tm),
        in_specs=[pl.BlockSpec((tm, d), lambda j, i: (i, 0)),
                  pl.BlockSpec((1, d), lambda j, i: (0, 0)),
                  pl.BlockSpec((d, tn), lambda j, i: (0, j))],
        out_specs=pl.BlockSpec((tm, tn), lambda j, i: (i, j)),
        out_shape=jax.ShapeDtypeStruct((n, m), F32),
        compiler_params=pltpu.CompilerParams(dimension_semantics=("arbitrary", "arbitrary")),
        name="inproj",
    )(x, gain, w)


def _gdn_kernel(q_ref, k_ref, v_ref, z_ref, sm_ref, cw_ref, alog_ref, dtb_ref, onorm_ref, o_ref,
                state_s, halo_s, mask_s, u_s, w_s, qd_s, kdt_s, at_s, cd_s, *, tile):
    blk = GDN_BLOCK
    nblk = tile // blk
    qk = GDN_HEADS * GDN_D
    n_levels = blk.bit_length() - 1
    row = lax.broadcasted_iota(jnp.int32, (blk, blk), 0)
    col = lax.broadcasted_iota(jnp.int32, (blk, blk), 1)

    @pl.when((pl.program_id(0) == 0) & (pl.program_id(1) == 0))
    def _():
        for lvl in range(n_levels):
            pair = (row >> (lvl + 1)) == (col >> (lvl + 1))
            mask_s[lvl] = (pair & (((row >> lvl) & 1) == 1) & (((col >> lvl) & 1) == 0)).astype(F32)

    @pl.when(pl.program_id(1) == 0)
    def _():
        state_s[...] = jnp.zeros_like(state_s)
        halo_s[...] = jnp.zeros_like(halo_s)

    lane_row = lax.broadcasted_iota(jnp.int32, (1, LANES), 1)
    neg_a = jnp.where(lane_row < GDN_HEADS, -jnp.exp(alog_ref[...]), 0.0)
    dtb = dtb_ref[...]
    srcs = (q_ref, k_ref, v_ref)

    def l2n(x):
        return x * lax.rsqrt(jnp.sum(x * x, axis=-1, keepdims=True) + EPS)

    def solve_phase(c, carry):
        r0 = pl.multiple_of(c * blk, blk)
        p0 = pl.multiple_of(jnp.maximum(r0 - SUBLANES, 0), SUBLANES)
        first = c == 0
        incl = row >= col
        strict = row > col
        eye_f = (row == col).astype(F32)
        tril = incl.astype(F32).astype(BF16)
        sm = sm_ref[pl.ds(r0, blk), :]
        xa = sm + dtb
        g = neg_a * (jnp.maximum(xa, 0.0) + jnp.log(1.0 + jnp.exp(-jnp.abs(xa))))
        beta_all = _sigmoid(sm)
        g_hi = g.astype(BF16)
        g_r = g - g_hi.astype(F32)
        g_mid = g_r.astype(BF16)
        g_lo = (g_r - g_mid.astype(F32)).astype(BF16)
        gcum = _dot(tril, g_hi) + (_dot(tril, g_mid) + _dot(tril, g_lo))
        gcum_t = gcum.T
        g_last = gcum[blk - 1:blk, :]
        e_g = jnp.exp(gcum)
        e_rest = jnp.exp(g_last - gcum)
        cd_s[pl.ds(c, 1), :] = jnp.exp(g_last)
        rows = pl.ds(r0, blk)
        heads = range(GDN_HEADS)

        def conv_silu(i, h):
            hs = slice(h * GDN_D, (h + 1) * GDN_D)
            cs = slice(i * qk + h * GDN_D, i * qk + (h + 1) * GDN_D)
            cur = srcs[i][rows, hs]
            prev = jnp.where(first, halo_s[:, cs], srcs[i][pl.ds(p0, SUBLANES), hs])
            ext = jnp.concatenate([prev, cur], axis=0)
            cw = cw_ref[:, cs]
            y = cur * cw[CONV_K - 1:CONV_K, :]
            for s in range(1, CONV_K):
                y = y + pltpu.roll(ext, s, 0)[SUBLANES:, :] * cw[CONV_K - 1 - s:CONV_K - s, :]
            return _silu(y)

        k = [l2n(conv_silu(1, h)) for h in heads]
        k_b = [k[h].astype(BF16) for h in heads]
        beta = [beta_all[:, GDN_HEADS + h:GDN_HEADS + h + 1] for h in heads]
        kb = [k[h] * beta[h] for h in heads]
        decay = []
        for h in heads:
            gc = jnp.broadcast_to(gcum[:, h:h + 1], (blk, blk))
            gr = jnp.broadcast_to(gcum_t[h:h + 1, :], (blk, blk))
            decay.append(jnp.exp(jnp.where(incl, gc - gr, -jnp.inf)))
        lower = [jnp.where(strict, _dot_nt(kb[h].astype(BF16), k_b[h]) * decay[h], 0.0) for h in heads]
        mask = mask_s[0]
        inv = [eye_f - lower[h] * mask for h in heads]
        for lvl in range(1, n_levels):
            mask = mask_s[lvl]
            inv_b = [inv[h].astype(BF16) for h in heads]
            half = [_mm(inv_b[h], lower[h] * mask).astype(BF16) for h in heads]
            inv = [inv[h] - _dot(half[h], inv_b[h]) for h in heads]
        inv_b = [inv[h].astype(BF16) for h in heads]
        for h in heads:
            eg = e_g[:, h:h + 1]
            q = l2n(conv_silu(0, h)) * (GDN_D ** -0.5)
            u_s[h, rows, :] = _mm(inv_b[h], conv_silu(2, h) * beta[h])
            w_s[h, rows, :] = _mm(inv_b[h], kb[h] * eg).astype(BF16)
            qd_s[h, rows, :] = (q * eg).astype(BF16)
            kdt_s[h, rows, :] = (k[h] * e_rest[:, h:h + 1]).T.astype(BF16)
            at_s[h, rows, :] = (_dot_nt(q.astype(BF16), k_b[h]) * decay[h]).astype(BF16)
        return carry

    lax.fori_loop(0, nblk, solve_phase, 0)

    onorm = onorm_ref[...]

    def scan_phase(c, carry):
        r0 = pl.multiple_of(c * blk, blk)
        rows = pl.ds(r0, blk)
        cd = cd_s[pl.ds(c, 1), :]
        heads = range(GDN_HEADS)
        state = [state_s[h] for h in heads]
        state_b = [state[h].astype(BF16) for h in heads]
        v_new = [u_s[h, rows, :] - _dot(w_s[h, rows, :], state_b[h]) for h in heads]
        v_new_b = [v_new[h].astype(BF16) for h in heads]
        for h in heads:
            state_s[h] = state[h] * cd[:, h:h + 1] + _dot(kdt_s[h, rows, :], v_new_b[h])
        for h in heads:
            hs = slice(h * GDN_D, (h + 1) * GDN_D)
            o = _dot(qd_s[h, rows, :], state_b[h]) + _dot(at_s[h, rows, :], v_new_b[h])
            o = _rms(o, onorm) * _silu(z_ref[rows, hs])
            o_ref[rows, hs] = o.astype(o_ref.dtype)
        return carry

    lax.fori_loop(0, nblk, scan_phase, 0)

    for i in range(3):
        halo_s[:, i * qk:(i + 1) * qk] = srcs[i][tile - SUBLANES:tile, :]


def _gdn(proj3, conv_w, a_log_row, dt_bias_row, out_norm):
    b, s, _ = proj3.shape
    tile = min(GDN_TILE, s)
    qk = GDN_HEADS * GDN_D
    heads_blk = qk // LANES

    def cols(c0):
        return pl.BlockSpec((None, tile, qk), lambda i, t: (i, t, c0 // heads_blk))

    row = lambda width: pl.BlockSpec((1, width), lambda i, t: (0, 0))
    per_head = lambda dtype: pltpu.VMEM((GDN_HEADS, tile, GDN_D), dtype)
    return pl.pallas_call(
        functools.partial(_gdn_kernel, tile=tile),
        grid=(b, s // tile),
        in_specs=[cols(COL_Q), cols(COL_K), cols(COL_V), cols(COL_Z),
                  pl.BlockSpec((None, tile, LANES), lambda i, t: (i, t, COL_SMALL)),
                  pl.BlockSpec((CONV_K, 3 * qk), lambda i, t: (0, 0)),
                  row(LANES), row(LANES), row(GDN_D)],
        out_specs=pl.BlockSpec((None, tile, qk), lambda i, t: (i, t, 0)),
        out_shape=jax.ShapeDtypeStruct((b, s, qk), BF16),
        scratch_shapes=[pltpu.VMEM((GDN_HEADS, GDN_D, GDN_D), F32),
                        pltpu.VMEM((SUBLANES, 3 * qk), F32),
                        pltpu.VMEM((GDN_BLOCK.bit_length() - 1, GDN_BLOCK, GDN_BLOCK), F32),
                        per_head(F32), per_head(BF16), per_head(BF16), per_head(BF16), per_head(BF16),
                        pltpu.VMEM((max(tile // GDN_BLOCK, SUBLANES), LANES), F32)],
        compiler_params=pltpu.CompilerParams(dimension_semantics=("arbitrary", "arbitrary")),
        name="gdn",
    )(proj3, proj3, proj3, proj3, proj3, conv_w, a_log_row, dt_bias_row, out_norm)


def _mla_prep_kernel(cqa_ref, ckv_ref, pos_ref, invf_ref, qn_ref, kvn_ref, wq_ref, wk_ref, wv_ref,
                     q_ref, k_ref, v_ref):
    cqa = cqa_ref[...]
    c_q = cqa[:, :Q_LORA]
    small = cqa[:, Q_LORA:]
    lane = lax.broadcasted_iota(jnp.int32, small.shape, 1)
    ang = pos_ref[...] * invf_ref[...]
    cosv = jnp.cos(ang)
    sinv = jnp.sin(ang)
    half = MLA_ROPE // 2
    lo = (lane >= ROPE_LANE0) & (lane < ROPE_LANE0 + half)
    hi = (lane >= ROPE_LANE0 + half) & (lane < ROPE_LANE0 + MLA_ROPE)
    sin_lo = jnp.where(lo, -sinv, 0.0)
    sin_hi = jnp.where(hi, sinv, 0.0)

    def rotate(t, cos_t):
        return t * cos_t + (pltpu.roll(t, LANES - half, 1) * sin_lo + pltpu.roll(t, half, 1) * sin_hi)

    scale = (MLA_NOPE + MLA_ROPE) ** -0.5 * LOG2_E
    q = _dot(_rms(c_q, qn_ref[...]).astype(BF16), wq_ref[...])
    cos_q = jnp.where(lo | hi, cosv, 1.0)
    hkv = _rms(ckv_ref[...], kvn_ref[...]).astype(BF16)
    k_nope = _dot(hkv, wk_ref[...])
    k_pe = rotate(small, jnp.where(lo | hi, cosv, 0.0))
    for h in range(MLA_HEADS):
        sl = slice(h * LANES, (h + 1) * LANES)
        q_ref[:, sl] = (rotate(q[:, sl], cos_q) * scale).astype(q_ref.dtype)
        k_ref[:, sl] = (k_nope[:, sl] + k_pe).astype(k_ref.dtype)
    v_ref[...] = _dot(hkv, wv_ref[...]).astype(v_ref.dtype)


def _mla_prep(proj, pos, invf, q_a_norm, kv_a_norm, wq, wk, wv, *, tm):
    n = proj.shape[0]
    hq = MLA_HEADS * LANES
    hv = MLA_HEADS * MLA_V
    full = lambda shape: pl.BlockSpec(shape, lambda i: (0, 0))
    return pl.pallas_call(
        _mla_prep_kernel,
        grid=(n // tm,),
        in_specs=[pl.BlockSpec((tm, Q_LORA + LANES), lambda i: (i, COL_CQ * LANES // (Q_LORA + LANES))),
                  pl.BlockSpec((tm, KV_LORA), lambda i: (i, COL_CKV * LANES // KV_LORA)),
                  pl.BlockSpec((tm, 1), lambda i: (i, 0)),
                  full((1, LANES)), full((1, Q_LORA)), full((1, KV_LORA)),
                  full((Q_LORA, hq)), full((KV_LORA, hq)), full((KV_LORA, hv))],
        out_specs=[pl.BlockSpec((tm, hq), lambda i: (i, 0)),
                   pl.BlockSpec((tm, hq), lambda i: (i, 0)),
                   pl.BlockSpec((tm, hv), lambda i: (i, 0))],
        out_shape=[jax.ShapeDtypeStruct((n, hq), BF16),
                   jax.ShapeDtypeStruct((n, hq), BF16),
                   jax.ShapeDtypeStruct((n, hv), BF16)],
        compiler_params=pltpu.CompilerParams(dimension_semantics=("arbitrary",)),
        name="mla_prep",
    )(proj, proj, pos, invf, q_a_norm, kv_a_norm, wq, wk, wv)


def _attn_kernel(q_ref, k_ref, v_ref, o_ref, s_s, m_s, l_s, acc_s, *, tq, tk):
    qi = pl.program_id(2)
    nsub = tq // tk
    lane_tiles = range(tk // LANES)
    heads = range(2)
    r_chunk = lax.broadcasted_iota(jnp.int32, (tq, tk), 0) // CHUNK
    c_chunk = lax.broadcasted_iota(jnp.int32, (tq, tk), 1) // CHUNK
    q = [q_ref[:, h * LANES:(h + 1) * LANES] for h in heads]
    m_s[...] = jnp.full(m_s.shape, -jnp.inf, F32)

    def scores(j, visible):
        k0 = pl.multiple_of(j * tk, tk)
        for h in heads:
            s = _dot_nt(q[h], k_ref[pl.ds(k0, tk), h * LANES:(h + 1) * LANES])
            if visible is not None:
                s = jnp.where(visible, s, -jnp.inf)
            s_s[h, j] = s
            part = m_s[h]
            for c in lane_tiles:
                part = jnp.maximum(part, s[:, c * LANES:(c + 1) * LANES])
            m_s[h] = part

    def scores_body(j, carry):
        scores(j, None)
        return carry

    lax.fori_loop(0, qi * nsub, scores_body, 0)
    for d in range(nsub):
        scores(qi * nsub + d, c_chunk + d * (tk // CHUNK) <= r_chunk)
    for h in heads:
        m_s[h] = jnp.broadcast_to(jnp.max(m_s[h], axis=-1, keepdims=True), (tq, LANES))
    l_s[...] = jnp.zeros(l_s.shape, F32)
    acc_s[...] = jnp.zeros(acc_s.shape, F32)

    def accumulate(j, carry):
        k0 = pl.multiple_of(j * tk, tk)
        v = v_ref[pl.ds(k0, tk), :]
        for h in heads:
            m = m_s[h]
            p = [jnp.exp2(s_s[h, j, :, c * LANES:(c + 1) * LANES] - m) for c in lane_tiles]
            l_s[h] += functools.reduce(lambda a, b: a + b, p)
            acc_s[h] += _dot(jnp.concatenate(p, axis=1).astype(BF16), v)
        return carry

    lax.fori_loop(0, (qi + 1) * nsub, accumulate, 0)
    outs = [acc_s[h] * (1.0 / jnp.sum(l_s[h], axis=-1, keepdims=True)) for h in heads]
    lane = lax.broadcasted_iota(jnp.int32, (tq, LANES), 1)
    o_ref[...] = jnp.where(lane < MLA_V, outs[0], outs[1]).astype(o_ref.dtype)


def _attention(q3, k3, v3, *, tq, tk):
    b, s, _ = q3.shape
    row_stat = pltpu.VMEM((2, tq, LANES), F32)
    return pl.pallas_call(
        functools.partial(_attn_kernel, tq=tq, tk=tk),
        scratch_shapes=[pltpu.VMEM((2, s // tk, tq, tk), F32), row_stat, row_stat, row_stat],
        grid=(b, MLA_HEADS // 2, s // tq),
        in_specs=[pl.BlockSpec((None, tq, 2 * LANES), lambda i, p, j: (i, j, p)),
                  pl.BlockSpec((None, s, 2 * LANES), lambda i, p, j: (i, 0, p)),
                  pl.BlockSpec((None, s, 2 * MLA_V), lambda i, p, j: (i, 0, p))],
        out_specs=pl.BlockSpec((None, tq, 2 * MLA_V), lambda i, p, j: (i, j, p)),
        out_shape=jax.ShapeDtypeStruct((b, s, MLA_HEADS * MLA_V), BF16),
        compiler_params=pltpu.CompilerParams(dimension_semantics=("arbitrary",) * 3),
        name="attention",
    )(q3, k3, v3)


def _merge_kernel(x_ref, oa_ref, ob_ref, gate_ref, gb_ref, wa_ref, wb_ref, wo_ref, o_ref):
    d = x_ref.shape[1]
    g = _sigmoid(gate_ref[...] + gb_ref[...])
    merged = g[:, :d] * _dot(oa_ref[...], wa_ref[...]) + g[:, d:] * _dot(ob_ref[...], wb_ref[...])
    o_ref[...] = x_ref[...] + _dot(merged.astype(BF16), wo_ref[...])


def _merge(x, o_a, o_b, proj, gate_bias, w_a, w_b, w_o, *, tm):
    n, d = x.shape
    full = lambda a: pl.BlockSpec(a.shape, lambda i: (0, 0))
    return pl.pallas_call(
        _merge_kernel,
        grid=(n // tm,),
        in_specs=[pl.BlockSpec((tm, d), lambda i: (i, 0)),
                  pl.BlockSpec((tm, o_a.shape[1]), lambda i: (i, 0)),
                  pl.BlockSpec((tm, o_b.shape[1]), lambda i: (i, 0)),
                  pl.BlockSpec((tm, 2 * d), lambda i: (i, COL_GATE)),
                  full(gate_bias), full(w_a), full(w_b), full(w_o)],
        out_specs=pl.BlockSpec((tm, d), lambda i: (i, 0)),
        out_shape=jax.ShapeDtypeStruct((n, d), F32),
        compiler_params=pltpu.CompilerParams(dimension_semantics=("arbitrary",)),
        name="merge",
    )(x, o_a, o_b, proj, gate_bias, w_a, w_b, w_o)


def _ffn_kernel(*refs, routed, final_norm):
    refs = list(refs)
    x_ref, g_ref = refs[:2]
    refs = refs[2:]
    if routed:
        rw_ref = refs.pop(0)
    w1_ref, w3_ref, w2_ref = refs[:3]
    refs = refs[3:]
    if final_norm:
        fin_ref = refs.pop(0)
    o_ref, h_s, acc_s = refs[:3]
    if routed:
        comb_s = refs[3]
    e = pl.program_id(1)
    f = pl.program_id(2)
    lane = lax.broadcasted_iota(jnp.int32, (x_ref.shape[0], LANES), 1)

    @pl.when((e == 0) & (f == 0))
    def _():
        x = x_ref[...]
        h = _rms(x, g_ref[...])
        h_s[...] = h.astype(BF16)
        acc_s[...] = x
        if routed:
            lane_f = lane.astype(F32)
            logits = jnp.where(lane < N_EXPERTS, _dot(h, rw_ref[...], HIGHEST), -jnp.inf)
            m1 = jnp.max(logits, axis=-1, keepdims=True)
            i1 = jnp.min(jnp.where(logits == m1, lane_f, float(LANES)), axis=-1, keepdims=True)
            rest = jnp.where(lane_f == i1, -jnp.inf, logits)
            m2 = jnp.max(rest, axis=-1, keepdims=True)
            i2 = jnp.min(jnp.where(rest == m2, lane_f, float(LANES)), axis=-1, keepdims=True)
            t = jnp.exp(m2 - m1)
            comb_s[...] = (jnp.where(lane_f == i1, 1.0 / (1.0 + t), 0.0)
                           + jnp.where(lane_f == i2, t / (1.0 + t), 0.0))

    h = h_s[...]
    hid = _silu(_dot(h, w1_ref[...])) * _dot(h, w3_ref[...])
    y = _dot(hid.astype(BF16), w2_ref[...])
    if routed:
        y = y * jnp.sum(jnp.where(lane == e, comb_s[...], 0.0), axis=-1, keepdims=True)
    acc_s[...] += y

    @pl.when((e == pl.num_programs(1) - 1) & (f == pl.num_programs(2) - 1))
    def _():
        out = acc_s[...]
        if final_norm:
            out = _rms(out, fin_ref[...])
        o_ref[...] = out


def _ffn(x, gain, w1, w3, w2, *, tm, tf, router_w=None, final_gain=None):
    n, d = x.shape
    ne, _, dff = w1.shape
    routed = router_w is not None
    final_norm = final_gain is not None
    row = lambda shape: pl.BlockSpec(shape, lambda i, e, f: (0, 0))
    in_specs = [pl.BlockSpec((tm, d), lambda i, e, f: (i, 0)), row((1, d))]
    args = [x, gain]
    if routed:
        in_specs.append(row(router_w.shape))
        args.append(router_w)
    in_specs += [pl.BlockSpec((None, d, tf), lambda i, e, f: (e, 0, f)),
                 pl.BlockSpec((None, d, tf), lambda i, e, f: (e, 0, f)),
                 pl.BlockSpec((None, tf, d), lambda i, e, f: (e, f, 0))]
    args += [w1, w3, w2]
    if final_norm:
        in_specs.append(row((1, d)))
        args.append(final_gain)
    scratch = [pltpu.VMEM((tm, d), BF16), pltpu.VMEM((tm, d), F32)]
    if routed:
        scratch.append(pltpu.VMEM((tm, LANES), F32))
    return pl.pallas_call(
        functools.partial(_ffn_kernel, routed=routed, final_norm=final_norm),
        grid=(n // tm, ne, dff // tf),
        in_specs=in_specs,
        out_specs=pl.BlockSpec((tm, d), lambda i, e, f: (i, 0)),
        out_shape=jax.ShapeDtypeStruct((n, d), F32),
        scratch_shapes=scratch,
        compiler_params=pltpu.CompilerParams(dimension_semantics=("arbitrary",) * 3),
        name="moe" if routed else "ffn",
    )(*args)


def _pad_cols(w, width):
    return jnp.pad(w, ((0, 0), (0, width - w.shape[1])))


def _layout_w_in(w):
    qk = GDN_HEADS * GDN_D
    o = 0
    parts = {}
    for name, width in (("q", qk), ("k", qk), ("v", qk), ("z", qk), ("a", GDN_HEADS), ("b", GDN_HEADS),
                        ("cq", Q_LORA), ("ckv", KV_LORA), ("kr", MLA_ROPE), ("gate", 2 * w.shape[0])):
        parts[name] = w[:, o:o + width]
        o += width
    small = jnp.concatenate(
        [_pad_cols(jnp.concatenate([parts["a"], parts["b"]], axis=1), ROPE_LANE0),
         _pad_cols(parts["kr"], LANES - ROPE_LANE0)], axis=1)
    return jnp.concatenate([parts["gate"], parts["q"], parts["k"], parts["v"], parts["z"], parts["cq"],
                            small, parts["ckv"]], axis=1).astype(BF16)


def _layout_heads(w, widths, pick, pad_to):
    per = sum(widths)
    k = w.shape[0]
    w = w.reshape(k, MLA_HEADS, per)
    start = sum(widths[:pick[0]])
    stop = sum(widths[:pick[1]])
    seg = w[:, :, start:stop]
    seg = jnp.pad(seg, ((0, 0), (0, 0), (0, pad_to - (stop - start))))
    return seg.reshape(k, MLA_HEADS * pad_to).astype(BF16)


def kernel(x, positions, mix_norm, w_in, gate_bias, conv_w, a_log, dt_bias, gdn_out_norm, w_gdn_o,
           q_a_norm, w_q_b, kv_a_norm, w_kv_b, w_mla_o, w_out, ffn_norm, dense_w1, dense_w3, dense_w2,
           router_w, moe_w1, moe_w3, moe_w2, final_norm):
    b, s, d = x.shape
    n = b * s
    depth = w_in.shape[0]
    xf = x.reshape(n, d)
    pos = positions.astype(F32).reshape(n, 1)
    inv_freq = 1.0 / (ROPE_THETA ** (jnp.arange(0, MLA_ROPE, 2, dtype=F32) / MLA_ROPE))
    invf = jnp.zeros((1, LANES), F32).at[0, ROPE_LANE0:ROPE_LANE0 + MLA_ROPE].set(
        jnp.concatenate([inv_freq, inv_freq]))
    row = lambda v: v.reshape(1, -1)

    for l in range(depth):
        proj = _inproj(xf, row(mix_norm[l]), _layout_w_in(w_in[l]), tm=512, tn=IN_COLS_PAD // 2)
        o_a = _gdn(proj.reshape(b, s, IN_COLS_PAD), conv_w[l], _pad_cols(row(a_log[l]), LANES),
                   _pad_cols(row(dt_bias[l]), LANES), row(gdn_out_norm[l]))
        wq = _layout_heads(w_q_b[l], (MLA_NOPE, MLA_ROPE), (0, 2), LANES)
        wk = _layout_heads(w_kv_b[l], (MLA_NOPE, MLA_V), (0, 1), LANES)
        wv = _layout_heads(w_kv_b[l], (MLA_NOPE, MLA_V), (1, 2), MLA_V)
        q, k, v = _mla_prep(proj, pos, invf, row(q_a_norm[l]), row(kv_a_norm[l]), wq, wk, wv, tm=512)
        o_b = _attention(q.reshape(b, s, -1), k.reshape(b, s, -1), v.reshape(b, s, -1), tq=512, tk=256)
        xf = _merge(xf, o_a.reshape(n, -1), o_b.reshape(n, -1), proj, row(gate_bias[l]),
                    w_gdn_o[l].astype(BF16), w_mla_o[l].astype(BF16), w_out[l].astype(BF16), tm=512)
        fin = row(final_norm) if l == depth - 1 else None
        j = l // 2
        if l % 2 == 0:
            xf = _ffn(xf, row(ffn_norm[l]), dense_w1[j:j + 1].astype(BF16), dense_w3[j:j + 1].astype(BF16),
                      dense_w2[j:j + 1].astype(BF16), tm=512, tf=512, final_gain=fin)
        else:
            xf = _ffn(xf, row(ffn_norm[l]), moe_w1[j].astype(BF16), moe_w3[j].astype(BF16),
                      moe_w2[j].astype(BF16), tm=512, tf=896,
                      router_w=_pad_cols(router_w[j], LANES), final_gain=fin)
    return xf.reshape(b, s, d)
```

```python
import functools
import math

import jax
import jax.numpy as jnp
from jax import lax
from jax.experimental import pallas as pl
from jax.experimental.pallas import tpu as pltpu

F32 = jnp.float32
BF16 = jnp.bfloat16
HIGHEST = lax.Precision.HIGHEST

LANES = 128
SUBLANES = 8
EPS = 1e-6
LOG2_E = math.log2(math.e)
CHUNK = 64
GDN_HEADS = 4
GDN_D = 128
GDN_BLOCK = 128
GDN_TILE = 512
CONV_K = 4
MLA_HEADS = 8
MLA_NOPE = 64
MLA_ROPE = 32
MLA_V = 64
Q_LORA = 384
KV_LORA = 256
ROPE_THETA = 10000.0
N_EXPERTS = 8
MOE_TILE = 512
MOE_VMEM_LIMIT_BYTES = 56 * 1024 * 1024

COL_GATE = 0
COL_Q = 16
COL_K = 20
COL_V = 24
COL_Z = 28
COL_CQ = 32
COL_SMALL = 35
COL_CKV = 36
IN_COLS_PAD = 38 * LANES
ROPE_LANE0 = 64


def _dot(a, b, precision=None):
    return jnp.dot(a, b, preferred_element_type=F32, precision=precision)


def _dot_nt(a, b, precision=None):
    return lax.dot_general(a, b, (((1,), (1,)), ((), ())), preferred_element_type=F32,
                           precision=precision)


def _sigmoid(x):
    return 1.0 / (1.0 + jnp.exp(-x))


def _silu(x):
    return x * _sigmoid(x)


def _rms(x, gain):
    return x * lax.rsqrt(jnp.mean(x * x, axis=-1, keepdims=True) + EPS) * gain


def _mm(a, b, nt=False):
    dot = _dot_nt if nt else _dot
    return dot(a.astype(BF16), b.astype(BF16))


def _inproj_kernel(x_ref, g_ref, w_ref, o_ref):
    h = _rms(x_ref[...], g_ref[...]).astype(BF16)
    o_ref[...] = _dot(h, w_ref[...])


def _inproj(x, gain, w, *, tm, tn):
    n, d = x.shape
    m = w.shape[1]
    return pl.pallas_call(
        _inproj_kernel,
        grid=(m // tn, n // tm),
        in_specs=[pl.BlockSpec((tm, d), lambda j, i: (i, 0)),
                  pl.BlockSpec((1, d), lambda j, i: (0, 0)),
                  pl.BlockSpec((d, tn), lambda j, i: (0, j))],
        out_specs=pl.BlockSpec((tm, tn), lambda j, i: (i, j)),
        out_shape=jax.ShapeDtypeStruct((n, m), F32),
        compiler_params=pltpu.CompilerParams(dimension_semantics=("arbitrary", "arbitrary")),
        name="inproj",
    )(x, gain, w)


def _gdn_kernel(q_ref, k_ref, v_ref, z_ref, sm_ref, cw_ref, alog_ref, dtb_ref, onorm_ref, o_ref,
                state_s, halo_s, mask_s, u_s, w_s, qd_s, kdt_s, at_s, cd_s, *, tile):
    blk = GDN_BLOCK
    nblk = tile // blk
    qk = GDN_HEADS * GDN_D
    n_levels = blk.bit_length() - 1
    row = lax.broadcasted_iota(jnp.int32, (blk, blk), 0)
    col = lax.broadcasted_iota(jnp.int32, (blk, blk), 1)

    @pl.when((pl.program_id(0) == 0) & (pl.program_id(1) == 0))
    def _():
        for lvl in range(n_levels):
            pair = (row >> (lvl + 1)) == (col >> (lvl + 1))
            mask_s[lvl] = (pair & (((row >> lvl) & 1) == 1) & (((col >> lvl) & 1) == 0)).astype(F32)

    @pl.when(pl.program_id(1) == 0)
    def _():
        state_s[...] = jnp.zeros_like(state_s)
        halo_s[...] = jnp.zeros_like(halo_s)

    lane_row = lax.broadcasted_iota(jnp.int32, (1, LANES), 1)
    neg_a = jnp.where(lane_row < GDN_HEADS, -jnp.exp(alog_ref[...]), 0.0)
    dtb = dtb_ref[...]
    srcs = (q_ref, k_ref, v_ref)

    def l2n(x):
        return x * lax.rsqrt(jnp.sum(x * x, axis=-1, keepdims=True) + EPS)

    def solve_phase(c, carry):
        r0 = pl.multiple_of(c * blk, blk)
        p0 = pl.multiple_of(jnp.maximum(r0 - SUBLANES, 0), SUBLANES)
        first = c == 0
        incl = row >= col
        strict = row > col
        eye_f = (row == col).astype(F32)
        tril = incl.astype(F32).astype(BF16)
        sm = sm_ref[pl.ds(r0, blk), :]
        xa = sm + dtb
        g = neg_a * (jnp.maximum(xa, 0.0) + jnp.log(1.0 + jnp.exp(-jnp.abs(xa))))
        beta_all = _sigmoid(sm)
        g_hi = g.astype(BF16)
        g_r = g - g_hi.astype(F32)
        g_mid = g_r.astype(BF16)
        g_lo = (g_r - g_mid.astype(F32)).astype(BF16)
        gcum = _dot(tril, g_hi) + (_dot(tril, g_mid) + _dot(tril, g_lo))
        gcum_t = gcum.T
        g_last = gcum[blk - 1:blk, :]
        e_g = jnp.exp(gcum)
        e_rest = jnp.exp(g_last - gcum)
        cd_s[pl.ds(c, 1), :] = jnp.exp(g_last)
        rows = pl.ds(r0, blk)
        heads = range(GDN_HEADS)

        def conv_silu(i, h):
            hs = slice(h * GDN_D, (h + 1) * GDN_D)
            cs = slice(i * qk + h * GDN_D, i * qk + (h + 1) * GDN_D)
            cur = srcs[i][rows, hs]
            prev = jnp.where(first, halo_s[:, cs], srcs[i][pl.ds(p0, SUBLANES), hs])
            ext = jnp.concatenate([prev, cur], axis=0)
            cw = cw_ref[:, cs]
            y = cur * cw[CONV_K - 1:CONV_K, :]
            for s in range(1, CONV_K):
                y = y + pltpu.roll(ext, s, 0)[SUBLANES:, :] * cw[CONV_K - 1 - s:CONV_K - s, :]
            return _silu(y)

        k = [l2n(conv_silu(1, h)) for h in heads]
        k_b = [k[h].astype(BF16) for h in heads]
        beta = [beta_all[:, GDN_HEADS + h:GDN_HEADS + h + 1] for h in heads]
        kb = [k[h] * beta[h] for h in heads]
        decay = []
        for h in heads:
            gc = jnp.broadcast_to(gcum[:, h:h + 1], (blk, blk))
            gr = jnp.broadcast_to(gcum_t[h:h + 1, :], (blk, blk))
            decay.append(jnp.exp(jnp.where(incl, gc - gr, -jnp.inf)))
        lower = [jnp.where(strict, _dot_nt(kb[h].astype(BF16), k_b[h]) * decay[h], 0.0) for h in heads]
        mask = mask_s[0]
        inv = [eye_f - lower[h] * mask for h in heads]
        for lvl in range(1, n_levels):
            mask = mask_s[lvl]
            inv_b = [inv[h].astype(BF16) for h in heads]
            half = [_mm(inv_b[h], lower[h] * mask).astype(BF16) for h in heads]
            inv = [inv[h] - _dot(half[h], inv_b[h]) for h in heads]
        inv_b = [inv[h].astype(BF16) for h in heads]
        for h in heads:
            eg = e_g[:, h:h + 1]
            q = l2n(conv_silu(0, h)) * (GDN_D ** -0.5)
            u_s[h, rows, :] = _mm(inv_b[h], conv_silu(2, h) * beta[h])
            w_s[h, rows, :] = _mm(inv_b[h], kb[h] * eg).astype(BF16)
            qd_s[h, rows, :] = (q * eg).astype(BF16)
            kdt_s[h, rows, :] = (k[h] * e_rest[:, h:h + 1]).T.astype(BF16)
            at_s[h, rows, :] = (_dot_nt(q.astype(BF16), k_b[h]) * decay[h]).astype(BF16)
        return carry

    lax.fori_loop(0, nblk, solve_phase, 0)

    onorm = onorm_ref[...]

    def scan_phase(c, carry):
        r0 = pl.multiple_of(c * blk, blk)
        rows = pl.ds(r0, blk)
        cd = cd_s[pl.ds(c, 1), :]
        heads = range(GDN_HEADS)
        state = [state_s[h] for h in heads]
        state_b = [state[h].astype(BF16) for h in heads]
        v_new = [u_s[h, rows, :] - _dot(w_s[h, rows, :], state_b[h]) for h in heads]
        v_new_b = [v_new[h].astype(BF16) for h in heads]
        for h in heads:
            state_s[h] = state[h] * cd[:, h:h + 1] + _dot(kdt_s[h, rows, :], v_new_b[h])
        for h in heads:
            hs = slice(h * GDN_D, (h + 1) * GDN_D)
            o = _dot(qd_s[h, rows, :], state_b[h]) + _dot(at_s[h, rows, :], v_new_b[h])
            o = _rms(o, onorm) * _silu(z_ref[rows, hs])
            o_ref[rows, hs] = o.astype(o_ref.dtype)
        return carry

    lax.fori_loop(0, nblk, scan_phase, 0)

    for i in range(3):
        halo_s[:, i * qk:(i + 1) * qk] = srcs[i][tile - SUBLANES:tile, :]


def _gdn(proj3, conv_w, a_log_row, dt_bias_row, out_norm):
    b, s, _ = proj3.shape
    tile = min(GDN_TILE, s)
    qk = GDN_HEADS * GDN_D
    heads_blk = qk // LANES

    def cols(c0):
        return pl.BlockSpec((None, tile, qk), lambda i, t: (i, t, c0 // heads_blk))

    row = lambda width: pl.BlockSpec((1, width), lambda i, t: (0, 0))
    per_head = lambda dtype: pltpu.VMEM((GDN_HEADS, tile, GDN_D), dtype)
    return pl.pallas_call(
        functools.partial(_gdn_kernel, tile=tile),
        grid=(b, s // tile),
        in_specs=[cols(COL_Q), cols(COL_K), cols(COL_V), cols(COL_Z),
                  pl.BlockSpec((None, tile, LANES), lambda i, t: (i, t, COL_SMALL)),
                  pl.BlockSpec((CONV_K, 3 * qk), lambda i, t: (0, 0)),
                  row(LANES), row(LANES), row(GDN_D)],
        out_specs=pl.BlockSpec((None, tile, qk), lambda i, t: (i, t, 0)),
        out_shape=jax.ShapeDtypeStruct((b, s, qk), BF16),
        scratch_shapes=[pltpu.VMEM((GDN_HEADS, GDN_D, GDN_D), F32),
                        pltpu.VMEM((SUBLANES, 3 * qk), F32),
                        pltpu.VMEM((GDN_BLOCK.bit_length() - 1, GDN_BLOCK, GDN_BLOCK), F32),
                        per_head(F32), per_head(BF16), per_head(BF16), per_head(BF16), per_head(BF16),
                        pltpu.VMEM((max(tile // GDN_BLOCK, SUBLANES), LANES), F32)],
        compiler_params=pltpu.CompilerParams(dimension_semantics=("arbitrary", "arbitrary")),
        name="gdn",
    )(proj3, proj3, proj3, proj3, proj3, conv_w, a_log_row, dt_bias_row, out_norm)


def _mla_prep_kernel(cqa_ref, ckv_ref, pos_ref, invf_ref, qn_ref, kvn_ref, wq_ref, wk_ref, wv_ref,
                     q_ref, k_ref, v_ref):
    cqa = cqa_ref[...]
    c_q = cqa[:, :Q_LORA]
    small = cqa[:, Q_LORA:]
    lane = lax.broadcasted_iota(jnp.int32, small.shape, 1)
    ang = pos_ref[...] * invf_ref[...]
    cosv = jnp.cos(ang)
    sinv = jnp.sin(ang)
    half = MLA_ROPE // 2
    lo = (lane >= ROPE_LANE0) & (lane < ROPE_LANE0 + half)
    hi = (lane >= ROPE_LANE0 + half) & (lane < ROPE_LANE0 + MLA_ROPE)
    sin_lo = jnp.where(lo, -sinv, 0.0)
    sin_hi = jnp.where(hi, sinv, 0.0)

    def rotate(t, cos_t):
        return t * cos_t + (pltpu.roll(t, LANES - half, 1) * sin_lo + pltpu.roll(t, half, 1) * sin_hi)

    scale = (MLA_NOPE + MLA_ROPE) ** -0.5 * LOG2_E
    q = _dot(_rms(c_q, qn_ref[...]).astype(BF16), wq_ref[...])
    cos_q = jnp.where(lo | hi, cosv, 1.0)
    hkv = _rms(ckv_ref[...], kvn_ref[...]).astype(BF16)
    k_nope = _dot(hkv, wk_ref[...])
    k_pe = rotate(small, jnp.where(lo | hi, cosv, 0.0))
    for h in range(MLA_HEADS):
        sl = slice(h * LANES, (h + 1) * LANES)
        q_ref[:, sl] = (rotate(q[:, sl], cos_q) * scale).astype(q_ref.dtype)
        k_ref[:, sl] = (k_nope[:, sl] + k_pe).astype(k_ref.dtype)
    v_ref[...] = _dot(hkv, wv_ref[...]).astype(v_ref.dtype)


def _mla_prep(proj, pos, invf, q_a_norm, kv_a_norm, wq, wk, wv, *, tm):
    n = proj.shape[0]
    hq = MLA_HEADS * LANES
    hv = MLA_HEADS * MLA_V
    full = lambda shape: pl.BlockSpec(shape, lambda i: (0, 0))
    return pl.pallas_call(
        _mla_prep_kernel,
        grid=(n // tm,),
        in_specs=[pl.BlockSpec((tm, Q_LORA + LANES), lambda i: (i, COL_CQ * LANES // (Q_LORA + LANES))),
                  pl.BlockSpec((tm, KV_LORA), lambda i: (i, COL_CKV * LANES // KV_LORA)),
                  pl.BlockSpec((tm, 1), lambda i: (i, 0)),
                  full((1, LANES)), full((1, Q_LORA)), full((1, KV_LORA)),
                  full((Q_LORA, hq)), full((KV_LORA, hq)), full((KV_LORA, hv))],
        out_specs=[pl.BlockSpec((tm, hq), lambda i: (i, 0)),
                   pl.BlockSpec((tm, hq), lambda i: (i, 0)),
                   pl.BlockSpec((tm, hv), lambda i: (i, 0))],
        out_shape=[jax.ShapeDtypeStruct((n, hq), BF16),
                   jax.ShapeDtypeStruct((n, hq), BF16),
                   jax.ShapeDtypeStruct((n, hv), BF16)],
        compiler_params=pltpu.CompilerParams(dimension_semantics=("arbitrary",)),
        name="mla_prep",
    )(proj, proj, pos, invf, q_a_norm, kv_a_norm, wq, wk, wv)


def _attn_kernel(q_ref, k_ref, v_ref, o_ref, s_s, m_s, l_s, acc_s, *, tq, tk):
    qi = pl.program_id(2)
    nsub = tq // tk
    lane_tiles = range(tk // LANES)
    heads = range(2)
    r_chunk = lax.broadcasted_iota(jnp.int32, (tq, tk), 0) // CHUNK
    c_chunk = lax.broadcasted_iota(jnp.int32, (tq, tk), 1) // CHUNK
    q = [q_ref[:, h * LANES:(h + 1) * LANES] for h in heads]
    m_s[...] = jnp.full(m_s.shape, -jnp.inf, F32)

    def scores(j, visible):
        k0 = pl.multiple_of(j * tk, tk)
        for h in heads:
            s = _dot_nt(q[h], k_ref[pl.ds(k0, tk), h * LANES:(h + 1) * LANES])
            if visible is not None:
                s = jnp.where(visible, s, -jnp.inf)
            s_s[h, j] = s
            part = m_s[h]
            for c in lane_tiles:
                part = jnp.maximum(part, s[:, c * LANES:(c + 1) * LANES])
            m_s[h] = part

    def scores_body(j, carry):
        scores(j, None)
        return carry

    lax.fori_loop(0, qi * nsub, scores_body, 0)
    for d in range(nsub):
        scores(qi * nsub + d, c_chunk + d * (tk // CHUNK) <= r_chunk)
    for h in heads:
        m_s[h] = jnp.broadcast_to(jnp.max(m_s[h], axis=-1, keepdims=True), (tq, LANES))
    l_s[...] = jnp.zeros(l_s.shape, F32)
    acc_s[...] = jnp.zeros(acc_s.shape, F32)

    def accumulate(j, carry):
        k0 = pl.multiple_of(j * tk, tk)
        v = v_ref[pl.ds(k0, tk), :]
        for h in heads:
            m = m_s[h]
            p = [jnp.exp2(s_s[h, j, :, c * LANES:(c + 1) * LANES] - m) for c in lane_tiles]
            l_s[h] += functools.reduce(lambda a, b: a + b, p)
            acc_s[h] += _dot(jnp.concatenate(p, axis=1).astype(BF16), v)
        return carry

    lax.fori_loop(0, (qi + 1) * nsub, accumulate, 0)
    outs = [acc_s[h] * (1.0 / jnp.sum(l_s[h], axis=-1, keepdims=True)) for h in heads]
    lane = lax.broadcasted_iota(jnp.int32, (tq, LANES), 1)
    o_ref[...] = jnp.where(lane < MLA_V, outs[0], outs[1]).astype(o_ref.dtype)


def _attention(q3, k3, v3, *, tq, tk):
    b, s, _ = q3.shape
    row_stat = pltpu.VMEM((2, tq, LANES), F32)
    return pl.pallas_call(
        functools.partial(_attn_kernel, tq=tq, tk=tk),
        scratch_shapes=[pltpu.VMEM((2, s // tk, tq, tk), F32), row_stat, row_stat, row_stat],
        grid=(b, MLA_HEADS // 2, s // tq),
        in_specs=[pl.BlockSpec((None, tq, 2 * LANES), lambda i, p, j: (i, j, p)),
                  pl.BlockSpec((None, s, 2 * LANES), lambda i, p, j: (i, 0, p)),
                  pl.BlockSpec((None, s, 2 * MLA_V), lambda i, p, j: (i, 0, p))],
        out_specs=pl.BlockSpec((None, tq, 2 * MLA_V), lambda i, p, j: (i, j, p)),
        out_shape=jax.ShapeDtypeStruct((b, s, MLA_HEADS * MLA_V), BF16),
        compiler_params=pltpu.CompilerParams(dimension_semantics=("arbitrary",) * 3),
        name="attention",
    )(q3, k3, v3)


def _merge_kernel(x_ref, oa_ref, ob_ref, gate_ref, gb_ref, wa_ref, wb_ref, wo_ref, o_ref):
    d = x_ref.shape[1]
    g = _sigmoid(gate_ref[...] + gb_ref[...])
    merged = g[:, :d] * _dot(oa_ref[...], wa_ref[...]) + g[:, d:] * _dot(ob_ref[...], wb_ref[...])
    o_ref[...] = x_ref[...] + _dot(merged.astype(BF16), wo_ref[...])


def _merge(x, o_a, o_b, proj, gate_bias, w_a, w_b, w_o, *, tm):
    n, d = x.shape
    full = lambda a: pl.BlockSpec(a.shape, lambda i: (0, 0))
    return pl.pallas_call(
        _merge_kernel,
        grid=(n // tm,),
        in_specs=[pl.BlockSpec((tm, d), lambda i: (i, 0)),
                  pl.BlockSpec((tm, o_a.shape[1]), lambda i: (i, 0)),
                  pl.BlockSpec((tm, o_b.shape[1]), lambda i: (i, 0)),
                  pl.BlockSpec((tm, 2 * d), lambda i: (i, COL_GATE)),
                  full(gate_bias), full(w_a), full(w_b), full(w_o)],
        out_specs=pl.BlockSpec((tm, d), lambda i: (i, 0)),
        out_shape=jax.ShapeDtypeStruct((n, d), F32),
        compiler_params=pltpu.CompilerParams(dimension_semantics=("arbitrary",)),
        name="merge",
    )(x, o_a, o_b, proj, gate_bias, w_a, w_b, w_o)


def _ffn_kernel(*refs, final_norm):
    if final_norm:
        x_ref, g_ref, w1_ref, w3_ref, w2_ref, fin_ref, o_ref, h_s, acc_s = refs
    else:
        x_ref, g_ref, w1_ref, w3_ref, w2_ref, o_ref, h_s, acc_s = refs
    f = pl.program_id(1)

    @pl.when(f == 0)
    def _():
        x = x_ref[...]
        h_s[...] = _rms(x, g_ref[...]).astype(BF16)
        acc_s[...] = x

    h = h_s[...]
    hid = _silu(_dot(h, w1_ref[...])) * _dot(h, w3_ref[...])
    acc_s[...] += _dot(hid.astype(BF16), w2_ref[...])

    @pl.when(f == pl.num_programs(1) - 1)
    def _():
        out = acc_s[...]
        if final_norm:
            out = _rms(out, fin_ref[...])
        o_ref[...] = out


def _ffn(x, gain, w1, w3, w2, *, tm, tf, final_gain=None):
    n, d = x.shape
    dff = w1.shape[1]
    final_norm = final_gain is not None
    row = pl.BlockSpec((1, d), lambda i, f: (0, 0))
    in_specs = [pl.BlockSpec((tm, d), lambda i, f: (i, 0)), row,
                pl.BlockSpec((d, tf), lambda i, f: (0, f)),
                pl.BlockSpec((d, tf), lambda i, f: (0, f)),
                pl.BlockSpec((tf, d), lambda i, f: (f, 0))]
    args = [x, gain, w1, w3, w2]
    if final_norm:
        in_specs.append(row)
        args.append(final_gain)
    return pl.pallas_call(
        functools.partial(_ffn_kernel, final_norm=final_norm),
        grid=(n // tm, dff // tf),
        in_specs=in_specs,
        out_specs=pl.BlockSpec((tm, d), lambda i, f: (i, 0)),
        out_shape=jax.ShapeDtypeStruct((n, d), F32),
        scratch_shapes=[pltpu.VMEM((tm, d), BF16), pltpu.VMEM((tm, d), F32)],
        compiler_params=pltpu.CompilerParams(dimension_semantics=("arbitrary", "arbitrary")),
        name="ffn",
    )(*args)


ROUTE_EXPERT = 0
ROUTE_RANK = 2
ROUTE_WEIGHT = 4


def _router_kernel(x_ref, g_ref, rw_ref, route_ref, count_ref, count_s, tri_s):
    tm = x_ref.shape[0]

    @pl.when(pl.program_id(0) == 0)
    def _():
        count_s[...] = jnp.zeros_like(count_s)
        r = lax.broadcasted_iota(jnp.int32, (tm, tm), 0)
        c = lax.broadcasted_iota(jnp.int32, (tm, tm), 1)
        tri_s[...] = (r > c).astype(F32).astype(BF16)

    lane = lax.broadcasted_iota(jnp.int32, (tm, LANES), 1)
    lane_f = lane.astype(F32)
    h = _rms(x_ref[...], g_ref[...])
    logits = jnp.where(lane < N_EXPERTS, _dot(h, rw_ref[...], HIGHEST), -jnp.inf)
    m1 = jnp.max(logits, axis=-1, keepdims=True)
    i1 = jnp.min(jnp.where(logits == m1, lane_f, float(LANES)), axis=-1, keepdims=True)
    rest = jnp.where(lane_f == i1, -jnp.inf, logits)
    m2 = jnp.max(rest, axis=-1, keepdims=True)
    i2 = jnp.min(jnp.where(rest == m2, lane_f, float(LANES)), axis=-1, keepdims=True)
    t = jnp.exp(m2 - m1)
    sel1 = lane_f == i1
    sel2 = lane_f == i2
    chosen = jnp.where(sel1 | sel2, 1.0, 0.0)
    before = _dot(tri_s[...], chosen.astype(BF16)) + count_s[...]
    r1 = jnp.sum(jnp.where(sel1, before, 0.0), axis=-1, keepdims=True)
    r2 = jnp.sum(jnp.where(sel2, before, 0.0), axis=-1, keepdims=True)
    count_s[...] += jnp.sum(chosen, axis=0, keepdims=True)
    fields = (i1, i2, r1, r2, 1.0 / (1.0 + t), t / (1.0 + t))
    route = jnp.zeros((tm, LANES), F32)
    for k, val in enumerate(fields):
        route = jnp.where(lane == k, val, route)
    route_ref[...] = route
    count_ref[...] = jnp.broadcast_to(count_s[...], count_ref.shape)


def _router(x, gain, router_w, *, tm):
    n, d = x.shape
    return pl.pallas_call(
        _router_kernel,
        grid=(n // tm,),
        in_specs=[pl.BlockSpec((tm, d), lambda i: (i, 0)),
                  pl.BlockSpec((1, d), lambda i: (0, 0)),
                  pl.BlockSpec((d, LANES), lambda i: (0, 0))],
        out_specs=[pl.BlockSpec((tm, LANES), lambda i: (i, 0)),
                   pl.BlockSpec((SUBLANES, LANES), lambda i: (0, 0))],
        out_shape=[jax.ShapeDtypeStruct((n, LANES), F32),
                   jax.ShapeDtypeStruct((SUBLANES, LANES), F32)],
        scratch_shapes=[pltpu.VMEM((1, LANES), F32), pltpu.VMEM((tm, tm), BF16)],
        compiler_params=pltpu.CompilerParams(dimension_semantics=("arbitrary",)),
        name="router",
    )(x, gain, router_w)


def _for_each_row_copy(dest_ref, src_at, dst_at, sem, n_rows, action):
    def body(r, carry):
        for s in range(2):
            d = dest_ref[0, 2 * r + s]
            action(pltpu.make_async_copy(src_at(r, s, d), dst_at(r, s, d), sem))
        return carry

    lax.fori_loop(0, n_rows, body, 0)


def _dispatch_kernel(dest_ref, x_ref, grouped_in_ref, grouped_ref, sem):
    del grouped_in_ref
    src_at = lambda r, s, d: x_ref.at[pl.ds(r, 1), :]
    dst_at = lambda r, s, d: grouped_ref.at[pl.ds(d, 1), :]
    _for_each_row_copy(dest_ref, src_at, dst_at, sem, x_ref.shape[0], lambda cp: cp.start())
    _for_each_row_copy(dest_ref, src_at, dst_at, sem, x_ref.shape[0], lambda cp: cp.wait())


def _dispatch(x, dest3, grouped_zeros, *, tm):
    n, d = x.shape
    return pl.pallas_call(
        _dispatch_kernel,
        grid=(n // tm,),
        in_specs=[pl.BlockSpec((None, 1, 2 * tm), lambda i: (i, 0, 0), memory_space=pltpu.SMEM),
                  pl.BlockSpec((tm, d), lambda i: (i, 0)),
                  pl.BlockSpec(memory_space=pl.ANY)],
        out_specs=pl.BlockSpec(memory_space=pl.ANY),
        out_shape=jax.ShapeDtypeStruct(grouped_zeros.shape, F32),
        scratch_shapes=[pltpu.SemaphoreType.DMA(())],
        input_output_aliases={2: 0},
        compiler_params=pltpu.CompilerParams(dimension_semantics=("arbitrary",)),
        name="dispatch",
    )(dest3, x, grouped_zeros)


def _experts_kernel(tile_expert_ref, used_ref, x_ref, g_ref, w1_ref, w3_ref, w2_ref, o_ref, *, tf):
    del tile_expert_ref
    i = pl.program_id(0)

    @pl.when(i < used_ref[0])
    def _():
        h = _rms(x_ref[...], g_ref[...]).astype(BF16)
        acc = jnp.zeros(o_ref.shape, F32)
        for f0 in range(0, w1_ref.shape[1], tf):
            hid = _silu(_dot(h, w1_ref[:, f0:f0 + tf])) * _dot(h, w3_ref[:, f0:f0 + tf])
            acc = acc + _dot(hid.astype(BF16), w2_ref[f0:f0 + tf, :])
        o_ref[...] = acc

    @pl.when(i >= used_ref[0])
    def _():
        o_ref[...] = jnp.zeros(o_ref.shape, F32)


def _experts(grouped, gain, w1, w3, w2, tile_expert, used, *, tile, tf):
    p, d = grouped.shape
    dff = w1.shape[2]
    row_tile = lambda i, te, used: (jnp.minimum(i, used[0] - 1), 0)
    expert = lambda i, te, used: (te[i], 0, 0)
    return pl.pallas_call(
        functools.partial(_experts_kernel, tf=tf),
        grid_spec=pltpu.PrefetchScalarGridSpec(
            num_scalar_prefetch=2,
            grid=(p // tile,),
            in_specs=[pl.BlockSpec((tile, d), row_tile),
                      pl.BlockSpec((1, d), lambda i, te, used: (0, 0)),
                      pl.BlockSpec((None, d, dff), expert),
                      pl.BlockSpec((None, d, dff), expert),
                      pl.BlockSpec((None, dff, d), expert)],
            out_specs=pl.BlockSpec((tile, d), lambda i, te, used: (i, 0))),
        out_shape=jax.ShapeDtypeStruct((p, d), F32),
        compiler_params=pltpu.CompilerParams(dimension_semantics=("arbitrary",),
                                             vmem_limit_bytes=MOE_VMEM_LIMIT_BYTES),
        name="experts",
    )(tile_expert, used, grouped, gain, w1, w3, w2)


def _combine_kernel(*refs, final_norm):
    if final_norm:
        dest_ref, x_ref, route_ref, fin_ref, y_ref, o_ref, ya_s, yb_s, sem = refs
    else:
        dest_ref, x_ref, route_ref, y_ref, o_ref, ya_s, yb_s, sem = refs
    bufs = (ya_s, yb_s)
    src_at = lambda r, s, d: y_ref.at[pl.ds(d, 1), :]
    dst_at = lambda r, s, d: bufs[s].at[pl.ds(r, 1), :]
    _for_each_row_copy(dest_ref, src_at, dst_at, sem, x_ref.shape[0], lambda cp: cp.start())
    _for_each_row_copy(dest_ref, src_at, dst_at, sem, x_ref.shape[0], lambda cp: cp.wait())
    route = route_ref[...]
    out = (x_ref[...] + route[:, ROUTE_WEIGHT:ROUTE_WEIGHT + 1] * ya_s[...]
           + route[:, ROUTE_WEIGHT + 1:ROUTE_WEIGHT + 2] * yb_s[...])
    if final_norm:
        out = _rms(out, fin_ref[...])
    o_ref[...] = out


def _combine(x, route, dest3, y, *, tm, final_gain=None):
    n, d = x.shape
    final_norm = final_gain is not None
    in_specs = [pl.BlockSpec((None, 1, 2 * tm), lambda i: (i, 0, 0), memory_space=pltpu.SMEM),
                pl.BlockSpec((tm, d), lambda i: (i, 0)),
                pl.BlockSpec((tm, LANES), lambda i: (i, 0))]
    args = [dest3, x, route]
    if final_norm:
        in_specs.append(pl.BlockSpec((1, d), lambda i: (0, 0)))
        args.append(final_gain)
    in_specs.append(pl.BlockSpec(memory_space=pl.ANY))
    args.append(y)
    return pl.pallas_call(
        functools.partial(_combine_kernel, final_norm=final_norm),
        grid=(n // tm,),
        in_specs=in_specs,
        out_specs=pl.BlockSpec((tm, d), lambda i: (i, 0)),
        out_shape=jax.ShapeDtypeStruct((n, d), F32),
        scratch_shapes=[pltpu.VMEM((tm, d), F32), pltpu.VMEM((tm, d), F32), pltpu.SemaphoreType.DMA(())],
        compiler_params=pltpu.CompilerParams(dimension_semantics=("arbitrary",)),
        name="combine",
    )(*args)


def _moe(x, gain, router_w, w1, w3, w2, *, tile, final_gain=None):
    n, d = x.shape
    max_tiles = 2 * n // tile + N_EXPERTS
    route, counts = _router(x, gain, router_w, tm=tile)
    counts = counts[0, :N_EXPERTS].astype(jnp.int32)
    tiles_per = (counts + tile - 1) // tile
    tile_end = jnp.cumsum(tiles_per)
    group_row0 = (tile_end - tiles_per) * tile
    expert = route[:, ROUTE_EXPERT:ROUTE_EXPERT + 2].astype(jnp.int32)
    rank = route[:, ROUTE_RANK:ROUTE_RANK + 2].astype(jnp.int32)
    dest3 = (group_row0[expert] + rank).reshape(n // tile, 1, 2 * tile)
    used = tile_end[-1:]
    tiles = jnp.minimum(jnp.arange(max_tiles, dtype=jnp.int32), used[0] - 1)
    tile_expert = jnp.sum(tiles[:, None] >= tile_end[None, :], axis=1).astype(jnp.int32)
    grouped = _dispatch(x, dest3, jnp.zeros((max_tiles * tile, d), F32), tm=tile)
    y = _experts(grouped, gain, w1, w3, w2, tile_expert, used, tile=tile, tf=w1.shape[2] // 2)
    return _combine(x, route, dest3, y, tm=tile, final_gain=final_gain)


def _pad_cols(w, width):
    return jnp.pad(w, ((0, 0), (0, width - w.shape[1])))


def _layout_w_in(w):
    qk = GDN_HEADS * GDN_D
    o = 0
    parts = {}
    for name, width in (("q", qk), ("k", qk), ("v", qk), ("z", qk), ("a", GDN_HEADS), ("b", GDN_HEADS),
                        ("cq", Q_LORA), ("ckv", KV_LORA), ("kr", MLA_ROPE), ("gate", 2 * w.shape[0])):
        parts[name] = w[:, o:o + width]
        o += width
    small = jnp.concatenate(
        [_pad_cols(jnp.concatenate([parts["a"], parts["b"]], axis=1), ROPE_LANE0),
         _pad_cols(parts["kr"], LANES - ROPE_LANE0)], axis=1)
    return jnp.concatenate([parts["gate"], parts["q"], parts["k"], parts["v"], parts["z"], parts["cq"],
                            small, parts["ckv"]], axis=1).astype(BF16)


def _layout_heads(w, widths, pick, pad_to):
    per = sum(widths)
    k = w.shape[0]
    w = w.reshape(k, MLA_HEADS, per)
    start = sum(widths[:pick[0]])
    stop = sum(widths[:pick[1]])
    seg = w[:, :, start:stop]
    seg = jnp.pad(seg, ((0, 0), (0, 0), (0, pad_to - (stop - start))))
    return seg.reshape(k, MLA_HEADS * pad_to).astype(BF16)


def kernel(x, positions, mix_norm, w_in, gate_bias, conv_w, a_log, dt_bias, gdn_out_norm, w_gdn_o,
           q_a_norm, w_q_b, kv_a_norm, w_kv_b, w_mla_o, w_out, ffn_norm, dense_w1, dense_w3, dense_w2,
           router_w, moe_w1, moe_w3, moe_w2, final_norm):
    b, s, d = x.shape
    n = b * s
    depth = w_in.shape[0]
    xf = x.reshape(n, d)
    pos = positions.astype(F32).reshape(n, 1)
    inv_freq = 1.0 / (ROPE_THETA ** (jnp.arange(0, MLA_ROPE, 2, dtype=F32) / MLA_ROPE))
    invf = jnp.zeros((1, LANES), F32).at[0, ROPE_LANE0:ROPE_LANE0 + MLA_ROPE].set(
        jnp.concatenate([inv_freq, inv_freq]))
    row = lambda v: v.reshape(1, -1)

    for l in range(depth):
        proj = _inproj(xf, row(mix_norm[l]), _layout_w_in(w_in[l]), tm=512, tn=IN_COLS_PAD // 2)
        o_a = _gdn(proj.reshape(b, s, IN_COLS_PAD), conv_w[l], _pad_cols(row(a_log[l]), LANES),
                   _pad_cols(row(dt_bias[l]), LANES), row(gdn_out_norm[l]))
        wq = _layout_heads(w_q_b[l], (MLA_NOPE, MLA_ROPE), (0, 2), LANES)
        wk = _layout_heads(w_kv_b[l], (MLA_NOPE, MLA_V), (0, 1), LANES)
        wv = _layout_heads(w_kv_b[l], (MLA_NOPE, MLA_V), (1, 2), MLA_V)
        q, k, v = _mla_prep(proj, pos, invf, row(q_a_norm[l]), row(kv_a_norm[l]), wq, wk, wv, tm=512)
        o_b = _attention(q.reshape(b, s, -1), k.reshape(b, s, -1), v.reshape(b, s, -1), tq=512, tk=256)
        xf = _merge(xf, o_a.reshape(n, -1), o_b.reshape(n, -1), proj, row(gate_bias[l]),
                    w_gdn_o[l].astype(BF16), w_mla_o[l].astype(BF16), w_out[l].astype(BF16), tm=512)
        fin = row(final_norm) if l == depth - 1 else None
        j = l // 2
        if l % 2 == 0:
            xf = _ffn(xf, row(ffn_norm[l]), dense_w1[j].astype(BF16), dense_w3[j].astype(BF16),
                      dense_w2[j].astype(BF16), tm=512, tf=512, final_gain=fin)
        else:
            xf = _moe(xf, row(ffn_norm[l]), _pad_cols(router_w[j], LANES), moe_w1[j].astype(BF16),
                      moe_w3[j].astype(BF16), moe_w2[j].astype(BF16), tile=MOE_TILE, final_gain=fin)
    return xf.reshape(b, s, d)
```

```python
import functools
import math

import jax
import jax.numpy as jnp
from jax import lax
from jax.experimental import pallas as pl
from jax.experimental.pallas import tpu as pltpu

F32 = jnp.float32
BF16 = jnp.bfloat16
HIGHEST = lax.Precision.HIGHEST

LANES = 128
SUBLANES = 8
MXU_DIM = 256
EPS = 1e-6
LOG2_E = math.log2(math.e)
CHUNK = 64
GDN_HEADS = 4
GDN_D = 128
GDN_BLOCK = 128
GDN_TILE = 512
GDN_SOLVE_BLOCKS = 2
CONV_K = 4
MLA_HEADS = 8
MLA_NOPE = 64
MLA_ROPE = 32
MLA_V = 64
Q_LORA = 384
KV_LORA = 256
ROPE_THETA = 10000.0
N_EXPERTS = 8
MOE_TILE = 512
MOE_VMEM_LIMIT_BYTES = 56 * 1024 * 1024

COL_GATE = 0
COL_Q = 16
COL_K = 20
COL_V = 24
COL_Z = 28
COL_CQ = 32
COL_SMALL = 35
COL_CKV = 36
IN_COLS_PAD = 38 * LANES
ROPE_LANE0 = 64


def _dot(a, b, precision=None):
    return jnp.dot(a, b, preferred_element_type=F32, precision=precision)


def _dot_nt(a, b, precision=None):
    return lax.dot_general(a, b, (((1,), (1,)), ((), ())), preferred_element_type=F32,
                           precision=precision)


def _sigmoid(x):
    return 1.0 / (1.0 + jnp.exp(-x))


def _silu(x):
    return x * _sigmoid(x)


def _rms(x, gain):
    return x * lax.rsqrt(jnp.mean(x * x, axis=-1, keepdims=True) + EPS) * gain


def _mm(a, b, nt=False):
    dot = _dot_nt if nt else _dot
    return dot(a.astype(BF16), b.astype(BF16))


def _inproj_kernel(x_ref, g_ref, w_ref, o_ref):
    h = _rms(x_ref[...], g_ref[...]).astype(BF16)
    o_ref[...] = _dot(h, w_ref[...])


def _inproj(x, gain, w, *, tm, tn):
    n, d = x.shape
    m = w.shape[1]
    return pl.pallas_call(
        _inproj_kernel,
        grid=(m // tn, n // tm),
        in_specs=[pl.BlockSpec((tm, d), lambda j, i: (i, 0)),
                  pl.BlockSpec((1, d), lambda j, i: (0, 0)),
                  pl.BlockSpec((d, tn), lambda j, i: (0, j))],
        out_specs=pl.BlockSpec((tm, tn), lambda j, i: (i, j)),
        out_shape=jax.ShapeDtypeStruct((n, m), F32),
        compiler_params=pltpu.CompilerParams(dimension_semantics=("arbitrary", "arbitrary")),
        name="inproj",
    )(x, gain, w)


def _gdn_kernel(q_ref, k_ref, v_ref, z_ref, sm_ref, cw_ref, alog_ref, dtb_ref, onorm_ref, o_ref,
                state_s, halo_s, mask_s, u_s, w_s, qd_s, kdt_s, at_s, cd_s, *, tile):
    blk = GDN_BLOCK
    nblk = tile // blk
    qk = GDN_HEADS * GDN_D
    n_levels = blk.bit_length() - 1
    row = lax.broadcasted_iota(jnp.int32, (blk, blk), 0)
    col = lax.broadcasted_iota(jnp.int32, (blk, blk), 1)

    @pl.when((pl.program_id(0) == 0) & (pl.program_id(1) == 0))
    def _():
        for lvl in range(n_levels):
            pair = (row >> (lvl + 1)) == (col >> (lvl + 1))
            mask_s[lvl] = (pair & (((row >> lvl) & 1) == 1) & (((col >> lvl) & 1) == 0)).astype(F32)

    @pl.when(pl.program_id(1) == 0)
    def _():
        state_s[...] = jnp.zeros_like(state_s)
        halo_s[...] = jnp.zeros_like(halo_s)

    lane_row = lax.broadcasted_iota(jnp.int32, (1, LANES), 1)
    neg_a = jnp.where(lane_row < GDN_HEADS, -jnp.exp(alog_ref[...]), 0.0)
    dtb = dtb_ref[...]
    srcs = (q_ref, k_ref, v_ref)

    def l2n(x):
        return x * lax.rsqrt(jnp.sum(x * x, axis=-1, keepdims=True) + EPS)

    def solve_phase(c2, carry):
        incl = row >= col
        strict = row > col
        eye_f = (row == col).astype(F32)
        tril = incl.astype(F32).astype(BF16)
        chains = [(b, h) for b in range(GDN_SOLVE_BLOCKS) for h in range(GDN_HEADS)]
        rows, gcum, gcum_t, e_g, e_rest, beta_all, conv_silu = {}, {}, {}, {}, {}, {}, {}
        for b in range(GDN_SOLVE_BLOCKS):
            c = c2 * GDN_SOLVE_BLOCKS + b
            r0 = pl.multiple_of(c * blk, blk)
            p0 = pl.multiple_of(jnp.maximum(r0 - SUBLANES, 0), SUBLANES)
            rows[b] = pl.ds(r0, blk)
            sm = sm_ref[rows[b], :]
            xa = sm + dtb
            g = neg_a * (jnp.maximum(xa, 0.0) + jnp.log(1.0 + jnp.exp(-jnp.abs(xa))))
            beta_all[b] = _sigmoid(sm)
            g_hi = g.astype(BF16)
            g_r = g - g_hi.astype(F32)
            g_mid = g_r.astype(BF16)
            g_lo = (g_r - g_mid.astype(F32)).astype(BF16)
            gcum[b] = _dot(tril, g_hi) + (_dot(tril, g_mid) + _dot(tril, g_lo))
            gcum_t[b] = gcum[b].T
            g_last = gcum[b][blk - 1:blk, :]
            e_g[b] = jnp.exp(gcum[b])
            e_rest[b] = jnp.exp(g_last - gcum[b])
            cd_s[pl.ds(c, 1), :] = jnp.exp(g_last)

            def conv_silu_b(i, h, first=c == 0, p0=p0, rows_b=rows[b]):
                hs = slice(h * GDN_D, (h + 1) * GDN_D)
                cs = slice(i * qk + h * GDN_D, i * qk + (h + 1) * GDN_D)
                cur = srcs[i][rows_b, hs]
                prev = jnp.where(first, halo_s[:, cs], srcs[i][pl.ds(p0, SUBLANES), hs])
                ext = jnp.concatenate([prev, cur], axis=0)
                cw = cw_ref[:, cs]
                y = cur * cw[CONV_K - 1:CONV_K, :]
                for s in range(1, CONV_K):
                    y = y + pltpu.roll(ext, s, 0)[SUBLANES:, :] * cw[CONV_K - 1 - s:CONV_K - s, :]
                return _silu(y)

            conv_silu[b] = conv_silu_b

        k = {ch: l2n(conv_silu[ch[0]](1, ch[1])) for ch in chains}
        k_b = {ch: k[ch].astype(BF16) for ch in chains}
        beta = {(b, h): beta_all[b][:, GDN_HEADS + h:GDN_HEADS + h + 1] for b, h in chains}
        kb = {ch: k[ch] * beta[ch] for ch in chains}
        decay = {}
        for b, h in chains:
            gc = jnp.broadcast_to(gcum[b][:, h:h + 1], (blk, blk))
            gr = jnp.broadcast_to(gcum_t[b][h:h + 1, :], (blk, blk))
            decay[b, h] = jnp.exp(jnp.where(incl, gc - gr, -jnp.inf))
        lower = {ch: jnp.where(strict, _dot_nt(kb[ch].astype(BF16), k_b[ch]) * decay[ch], 0.0)
                 for ch in chains}
        mask = mask_s[0]
        inv = {ch: eye_f - lower[ch] * mask for ch in chains}
        for lvl in range(1, n_levels):
            mask = mask_s[lvl]
            inv_b = {ch: inv[ch].astype(BF16) for ch in chains}
            half = {ch: _mm(inv_b[ch], lower[ch] * mask).astype(BF16) for ch in chains}
            inv = {ch: inv[ch] - _dot(half[ch], inv_b[ch]) for ch in chains}
        inv_b = {ch: inv[ch].astype(BF16) for ch in chains}
        for b, h in chains:
            ch = (b, h)
            eg = e_g[b][:, h:h + 1]
            q = l2n(conv_silu[b](0, h)) * (GDN_D ** -0.5)
            u_s[h, rows[b], :] = _mm(inv_b[ch], conv_silu[b](2, h) * beta[ch])
            w_s[h, rows[b], :] = _mm(inv_b[ch], kb[ch] * eg).astype(BF16)
            qd_s[h, rows[b], :] = (q * eg).astype(BF16)
            kdt_s[h, rows[b], :] = (k[ch] * e_rest[b][:, h:h + 1]).T.astype(BF16)
            at_s[h, rows[b], :] = (_dot_nt(q.astype(BF16), k_b[ch]) * decay[ch]).astype(BF16)
        return carry

    lax.fori_loop(0, nblk // GDN_SOLVE_BLOCKS, solve_phase, 0)

    onorm = onorm_ref[...]

    def scan_phase(c, carry):
        r0 = pl.multiple_of(c * blk, blk)
        rows = pl.ds(r0, blk)
        cd = cd_s[pl.ds(c, 1), :]
        heads = range(GDN_HEADS)
        state = [state_s[h] for h in heads]
        state_b = [state[h].astype(BF16) for h in heads]
        v_new = [u_s[h, rows, :] - _dot(w_s[h, rows, :], state_b[h]) for h in heads]
        v_new_b = [v_new[h].astype(BF16) for h in heads]
        for h in heads:
            state_s[h] = state[h] * cd[:, h:h + 1] + _dot(kdt_s[h, rows, :], v_new_b[h])
        for h in heads:
            hs = slice(h * GDN_D, (h + 1) * GDN_D)
            o = _dot(qd_s[h, rows, :], state_b[h]) + _dot(at_s[h, rows, :], v_new_b[h])
            o = _rms(o, onorm) * _silu(z_ref[rows, hs])
            o_ref[rows, hs] = o.astype(o_ref.dtype)
        return carry

    lax.fori_loop(0, nblk, scan_phase, 0)

    for i in range(3):
        halo_s[:, i * qk:(i + 1) * qk] = srcs[i][tile - SUBLANES:tile, :]


def _gdn(proj3, conv_w, a_log_row, dt_bias_row, out_norm):
    b, s, _ = proj3.shape
    tile = min(GDN_TILE, s)
    qk = GDN_HEADS * GDN_D
    heads_blk = qk // LANES

    def cols(c0):
        return pl.BlockSpec((None, tile, qk), lambda i, t: (i, t, c0 // heads_blk))

    row = lambda width: pl.BlockSpec((1, width), lambda i, t: (0, 0))
    per_head = lambda dtype: pltpu.VMEM((GDN_HEADS, tile, GDN_D), dtype)
    return pl.pallas_call(
        functools.partial(_gdn_kernel, tile=tile),
        grid=(b, s // tile),
        in_specs=[cols(COL_Q), cols(COL_K), cols(COL_V), cols(COL_Z),
                  pl.BlockSpec((None, tile, LANES), lambda i, t: (i, t, COL_SMALL)),
                  pl.BlockSpec((CONV_K, 3 * qk), lambda i, t: (0, 0)),
                  row(LANES), row(LANES), row(GDN_D)],
        out_specs=pl.BlockSpec((None, tile, qk), lambda i, t: (i, t, 0)),
        out_shape=jax.ShapeDtypeStruct((b, s, qk), BF16),
        scratch_shapes=[pltpu.VMEM((GDN_HEADS, GDN_D, GDN_D), F32),
                        pltpu.VMEM((SUBLANES, 3 * qk), F32),
                        pltpu.VMEM((GDN_BLOCK.bit_length() - 1, GDN_BLOCK, GDN_BLOCK), F32),
                        per_head(F32), per_head(BF16), per_head(BF16), per_head(BF16), per_head(BF16),
                        pltpu.VMEM((max(tile // GDN_BLOCK, SUBLANES), LANES), F32)],
        compiler_params=pltpu.CompilerParams(dimension_semantics=("arbitrary", "arbitrary")),
        name="gdn",
    )(proj3, proj3, proj3, proj3, proj3, conv_w, a_log_row, dt_bias_row, out_norm)


def _mla_prep_kernel(cqa_ref, ckv_ref, pos_ref, invf_ref, qn_ref, kvn_ref, wq_ref, wk_ref, wv_ref,
                     q_ref, k_ref, v_ref):
    cqa = cqa_ref[...]
    c_q = cqa[:, :Q_LORA]
    small = cqa[:, Q_LORA:]
    lane = lax.broadcasted_iota(jnp.int32, small.shape, 1)
    ang = pos_ref[...] * invf_ref[...]
    cosv = jnp.cos(ang)
    sinv = jnp.sin(ang)
    half = MLA_ROPE // 2
    lo = (lane >= ROPE_LANE0) & (lane < ROPE_LANE0 + half)
    hi = (lane >= ROPE_LANE0 + half) & (lane < ROPE_LANE0 + MLA_ROPE)
    sin_lo = jnp.where(lo, -sinv, 0.0)
    sin_hi = jnp.where(hi, sinv, 0.0)

    def rotate(t, cos_t):
        return t * cos_t + (pltpu.roll(t, LANES - half, 1) * sin_lo + pltpu.roll(t, half, 1) * sin_hi)

    scale = (MLA_NOPE + MLA_ROPE) ** -0.5 * LOG2_E
    q = _dot(_rms(c_q, qn_ref[...]).astype(BF16), wq_ref[...])
    cos_q = jnp.where(lo | hi, cosv, 1.0)
    hkv = _rms(ckv_ref[...], kvn_ref[...]).astype(BF16)
    k_nope = _dot(hkv, wk_ref[...])
    k_pe = rotate(small, jnp.where(lo | hi, cosv, 0.0))
    for h in range(MLA_HEADS):
        sl = slice(h * LANES, (h + 1) * LANES)
        q_ref[:, sl] = (rotate(q[:, sl], cos_q) * scale).astype(q_ref.dtype)
        k_ref[:, sl] = (k_nope[:, sl] + k_pe).astype(k_ref.dtype)
    v_ref[...] = _dot(hkv, wv_ref[...]).astype(v_ref.dtype)


def _mla_prep(proj, pos, invf, q_a_norm, kv_a_norm, wq, wk, wv, *, tm):
    n = proj.shape[0]
    hq = MLA_HEADS * LANES
    hv = MLA_HEADS * MLA_V
    full = lambda shape: pl.BlockSpec(shape, lambda i: (0, 0))
    return pl.pallas_call(
        _mla_prep_kernel,
        grid=(n // tm,),
        in_specs=[pl.BlockSpec((tm, Q_LORA + LANES), lambda i: (i, COL_CQ * LANES // (Q_LORA + LANES))),
                  pl.BlockSpec((tm, KV_LORA), lambda i: (i, COL_CKV * LANES // KV_LORA)),
                  pl.BlockSpec((tm, 1), lambda i: (i, 0)),
                  full((1, LANES)), full((1, Q_LORA)), full((1, KV_LORA)),
                  full((Q_LORA, hq)), full((KV_LORA, hq)), full((KV_LORA, hv))],
        out_specs=[pl.BlockSpec((tm, hq), lambda i: (i, 0)),
                   pl.BlockSpec((tm, hq), lambda i: (i, 0)),
                   pl.BlockSpec((tm, hv), lambda i: (i, 0))],
        out_shape=[jax.ShapeDtypeStruct((n, hq), BF16),
                   jax.ShapeDtypeStruct((n, hq), BF16),
                   jax.ShapeDtypeStruct((n, hv), BF16)],
        compiler_params=pltpu.CompilerParams(dimension_semantics=("arbitrary",)),
        name="mla_prep",
    )(proj, proj, pos, invf, q_a_norm, kv_a_norm, wq, wk, wv)


def _attn_kernel(q_ref, k_ref, v_ref, o_ref, s_s, m_s, l_s, acc_s, *, tq, tk):
    qi = pl.program_id(2)
    nsub = tq // tk
    lane_tiles = range(tk // LANES)
    heads = range(2)
    r_chunk = lax.broadcasted_iota(jnp.int32, (tq, tk), 0) // CHUNK
    c_chunk = lax.broadcasted_iota(jnp.int32, (tq, tk), 1) // CHUNK
    q = [q_ref[:, h * LANES:(h + 1) * LANES] for h in heads]
    m_s[...] = jnp.full(m_s.shape, -jnp.inf, F32)

    def scores(j, visible):
        k0 = pl.multiple_of(j * tk, tk)
        for h in heads:
            s = _dot_nt(q[h], k_ref[pl.ds(k0, tk), h * LANES:(h + 1) * LANES])
            if visible is not None:
                s = jnp.where(visible, s, -jnp.inf)
            s_s[h, j] = s
            part = m_s[h]
            for c in lane_tiles:
                part = jnp.maximum(part, s[:, c * LANES:(c + 1) * LANES])
            m_s[h] = part

    def scores_body(j, carry):
        scores(j, None)
        return carry

    lax.fori_loop(0, qi * nsub, scores_body, 0)
    for d in range(nsub):
        scores(qi * nsub + d, c_chunk + d * (tk // CHUNK) <= r_chunk)
    for h in heads:
        m_s[h] = jnp.broadcast_to(jnp.max(m_s[h], axis=-1, keepdims=True), (tq, LANES))
    l_s[...] = jnp.zeros(l_s.shape, F32)
    acc_s[...] = jnp.zeros(acc_s.shape, F32)

    def accumulate(j, carry):
        k0 = pl.multiple_of(j * tk, tk)
        v = v_ref[pl.ds(k0, tk), :]
        for h in heads:
            m = m_s[h]
            p = [jnp.exp2(s_s[h, j, :, c * LANES:(c + 1) * LANES] - m) for c in lane_tiles]
            l_s[h] += functools.reduce(lambda a, b: a + b, p)
            acc_s[h] += _dot(jnp.concatenate(p, axis=1).astype(BF16), v)
        return carry

    lax.fori_loop(0, (qi + 1) * nsub, accumulate, 0)
    outs = [acc_s[h] * (1.0 / jnp.sum(l_s[h], axis=-1, keepdims=True)) for h in heads]
    lane = lax.broadcasted_iota(jnp.int32, (tq, LANES), 1)
    o_ref[...] = jnp.where(lane < MLA_V, outs[0], outs[1]).astype(o_ref.dtype)


def _attention(q3, k3, v3, *, tq, tk):
    b, s, _ = q3.shape
    row_stat = pltpu.VMEM((2, tq, LANES), F32)
    return pl.pallas_call(
        functools.partial(_attn_kernel, tq=tq, tk=tk),
        scratch_shapes=[pltpu.VMEM((2, s // tk, tq, tk), F32), row_stat, row_stat, row_stat],
        grid=(b, MLA_HEADS // 2, s // tq),
        in_specs=[pl.BlockSpec((None, tq, 2 * LANES), lambda i, p, j: (i, j, p)),
                  pl.BlockSpec((None, s, 2 * LANES), lambda i, p, j: (i, 0, p)),
                  pl.BlockSpec((None, s, 2 * MLA_V), lambda i, p, j: (i, 0, p))],
        out_specs=pl.BlockSpec((None, tq, 2 * MLA_V), lambda i, p, j: (i, j, p)),
        out_shape=jax.ShapeDtypeStruct((b, s, MLA_HEADS * MLA_V), BF16),
        compiler_params=pltpu.CompilerParams(dimension_semantics=("arbitrary",) * 3),
        name="attention",
    )(q3, k3, v3)


def _merge_kernel(x_ref, oa_ref, ob_ref, gate_ref, gb_ref, wa_ref, wb_ref, wo_ref, o_ref):
    d = x_ref.shape[1]
    g = _sigmoid(gate_ref[...] + gb_ref[...])
    merged = g[:, :d] * _dot(oa_ref[...], wa_ref[...]) + g[:, d:] * _dot(ob_ref[...], wb_ref[...])
    o_ref[...] = x_ref[...] + _dot(merged.astype(BF16), wo_ref[...])


def _merge(x, o_a, o_b, proj, gate_bias, w_a, w_b, w_o, *, tm):
    n, d = x.shape
    full = lambda a: pl.BlockSpec(a.shape, lambda i: (0, 0))
    return pl.pallas_call(
        _merge_kernel,
        grid=(n // tm,),
        in_specs=[pl.BlockSpec((tm, d), lambda i: (i, 0)),
                  pl.BlockSpec((tm, o_a.shape[1]), lambda i: (i, 0)),
                  pl.BlockSpec((tm, o_b.shape[1]), lambda i: (i, 0)),
                  pl.BlockSpec((tm, 2 * d), lambda i: (i, COL_GATE)),
                  full(gate_bias), full(w_a), full(w_b), full(w_o)],
        out_specs=pl.BlockSpec((tm, d), lambda i: (i, 0)),
        out_shape=jax.ShapeDtypeStruct((n, d), F32),
        compiler_params=pltpu.CompilerParams(dimension_semantics=("arbitrary",)),
        name="merge",
    )(x, o_a, o_b, proj, gate_bias, w_a, w_b, w_o)


def _ffn_kernel(*refs, final_norm, sub):
    if final_norm:
        x_ref, g_ref, w1_ref, w3_ref, w2_ref, fin_ref, o_ref, h_s, hid_s, acc_s = refs
    else:
        x_ref, g_ref, w1_ref, w3_ref, w2_ref, o_ref, h_s, hid_s, acc_s = refs
    f = pl.program_id(1)

    @pl.when(f == 0)
    def _():
        x = x_ref[...]
        h_s[...] = _rms(x, g_ref[...]).astype(BF16)
        acc_s[...] = x

    h = h_s[...]
    for c0 in range(0, hid_s.shape[1], sub):
        cols = slice(c0, c0 + sub)
        hid_s[:, cols] = (_silu(_dot(h, w1_ref[:, cols])) * _dot(h, w3_ref[:, cols])).astype(BF16)
    acc_s[...] += _dot(hid_s[...], w2_ref[...])

    @pl.when(f == pl.num_programs(1) - 1)
    def _():
        out = acc_s[...]
        if final_norm:
            out = _rms(out, fin_ref[...])
        o_ref[...] = out


def _ffn(x, gain, w1, w3, w2, *, tm, tf, final_gain=None):
    n, d = x.shape
    dff = w1.shape[1]
    final_norm = final_gain is not None
    row = pl.BlockSpec((1, d), lambda i, f: (0, 0))
    in_specs = [pl.BlockSpec((tm, d), lambda i, f: (i, 0)), row,
                pl.BlockSpec((d, tf), lambda i, f: (0, f)),
                pl.BlockSpec((d, tf), lambda i, f: (0, f)),
                pl.BlockSpec((tf, d), lambda i, f: (f, 0))]
    args = [x, gain, w1, w3, w2]
    if final_norm:
        in_specs.append(row)
        args.append(final_gain)
    return pl.pallas_call(
        functools.partial(_ffn_kernel, final_norm=final_norm, sub=MXU_DIM),
        grid=(n // tm, dff // tf),
        in_specs=in_specs,
        out_specs=pl.BlockSpec((tm, d), lambda i, f: (i, 0)),
        out_shape=jax.ShapeDtypeStruct((n, d), F32),
        scratch_shapes=[pltpu.VMEM((tm, d), BF16), pltpu.VMEM((tm, tf), BF16), pltpu.VMEM((tm, d), F32)],
        compiler_params=pltpu.CompilerParams(dimension_semantics=("arbitrary", "arbitrary"),
                                             vmem_limit_bytes=MOE_VMEM_LIMIT_BYTES),
        name="ffn",
    )(*args)


ROUTE_EXPERT = 0
ROUTE_RANK = 2
ROUTE_WEIGHT = 4


def _router_kernel(x_ref, g_ref, rw_ref, route_ref, count_ref, count_s, tri_s):
    tm = x_ref.shape[0]

    @pl.when(pl.program_id(0) == 0)
    def _():
        count_s[...] = jnp.zeros_like(count_s)
        r = lax.broadcasted_iota(jnp.int32, (tm, tm), 0)
        c = lax.broadcasted_iota(jnp.int32, (tm, tm), 1)
        tri_s[...] = (r > c).astype(F32).astype(BF16)

    lane = lax.broadcasted_iota(jnp.int32, (tm, LANES), 1)
    lane_f = lane.astype(F32)
    h = _rms(x_ref[...], g_ref[...])
    logits = jnp.where(lane < N_EXPERTS, _dot(h, rw_ref[...], HIGHEST), -jnp.inf)
    m1 = jnp.max(logits, axis=-1, keepdims=True)
    i1 = jnp.min(jnp.where(logits == m1, lane_f, float(LANES)), axis=-1, keepdims=True)
    rest = jnp.where(lane_f == i1, -jnp.inf, logits)
    m2 = jnp.max(rest, axis=-1, keepdims=True)
    i2 = jnp.min(jnp.where(rest == m2, lane_f, float(LANES)), axis=-1, keepdims=True)
    t = jnp.exp(m2 - m1)
    sel1 = lane_f == i1
    sel2 = lane_f == i2
    chosen = jnp.where(sel1 | sel2, 1.0, 0.0)
    before = _dot(tri_s[...], chosen.astype(BF16)) + count_s[...]
    r1 = jnp.sum(jnp.where(sel1, before, 0.0), axis=-1, keepdims=True)
    r2 = jnp.sum(jnp.where(sel2, before, 0.0), axis=-1, keepdims=True)
    count_s[...] += jnp.sum(chosen, axis=0, keepdims=True)
    fields = (i1, i2, r1, r2, 1.0 / (1.0 + t), t / (1.0 + t))
    route = jnp.zeros((tm, LANES), F32)
    for k, val in enumerate(fields):
        route = jnp.where(lane == k, val, route)
    route_ref[...] = route
    count_ref[...] = jnp.broadcast_to(count_s[...], count_ref.shape)


def _router(x, gain, router_w, *, tm):
    n, d = x.shape
    return pl.pallas_call(
        _router_kernel,
        grid=(n // tm,),
        in_specs=[pl.BlockSpec((tm, d), lambda i: (i, 0)),
                  pl.BlockSpec((1, d), lambda i: (0, 0)),
                  pl.BlockSpec((d, LANES), lambda i: (0, 0))],
        out_specs=[pl.BlockSpec((tm, LANES), lambda i: (i, 0)),
                   pl.BlockSpec((SUBLANES, LANES), lambda i: (0, 0))],
        out_shape=[jax.ShapeDtypeStruct((n, LANES), F32),
                   jax.ShapeDtypeStruct((SUBLANES, LANES), F32)],
        scratch_shapes=[pltpu.VMEM((1, LANES), F32), pltpu.VMEM((tm, tm), BF16)],
        compiler_params=pltpu.CompilerParams(dimension_semantics=("arbitrary",)),
        name="router",
    )(x, gain, router_w)


def _start_copy(copy):
    copy.start()


def _wait_copy(copy):
    copy.wait()


def _for_each_row_copy(dest_ref, src_at, dst_at, sem, n_rows, action):
    def body(r, carry):
        for s in range(2):
            d = dest_ref[0, 2 * r + s]
            action(pltpu.make_async_copy(src_at(r, s, d), dst_at(r, s, d), sem))
        return carry

    lax.fori_loop(0, n_rows, body, 0)


def _dispatch_kernel(dest_ref, x_ref, grouped_in_ref, grouped_ref, sem):
    del grouped_in_ref
    src_at = lambda r, s, d: x_ref.at[pl.ds(r, 1), :]
    dst_at = lambda r, s, d: grouped_ref.at[pl.ds(d, 1), :]
    _for_each_row_copy(dest_ref, src_at, dst_at, sem, x_ref.shape[0], _start_copy)
    _for_each_row_copy(dest_ref, src_at, dst_at, sem, x_ref.shape[0], _wait_copy)


def _dispatch(x, dest3, grouped_zeros, *, tm):
    n, d = x.shape
    return pl.pallas_call(
        _dispatch_kernel,
        grid=(n // tm,),
        in_specs=[pl.BlockSpec((None, 1, 2 * tm), lambda i: (i, 0, 0), memory_space=pltpu.SMEM),
                  pl.BlockSpec((tm, d), lambda i: (i, 0)),
                  pl.BlockSpec(memory_space=pl.ANY)],
        out_specs=pl.BlockSpec(memory_space=pl.ANY),
        out_shape=jax.ShapeDtypeStruct(grouped_zeros.shape, F32),
        scratch_shapes=[pltpu.SemaphoreType.DMA(())],
        input_output_aliases={2: 0},
        compiler_params=pltpu.CompilerParams(dimension_semantics=("arbitrary",)),
        name="dispatch",
    )(dest3, x, grouped_zeros)


def _experts_kernel(tile_expert_ref, used_ref, x_ref, g_ref, w1_ref, w3_ref, w2_ref, o_ref, *, tf):
    del tile_expert_ref
    i = pl.program_id(0)

    @pl.when(i < used_ref[0])
    def _():
        h = _rms(x_ref[...], g_ref[...]).astype(BF16)
        acc = jnp.zeros(o_ref.shape, F32)
        for f0 in range(0, w1_ref.shape[1], tf):
            hid = _silu(_dot(h, w1_ref[:, f0:f0 + tf])) * _dot(h, w3_ref[:, f0:f0 + tf])
            acc = acc + _dot(hid.astype(BF16), w2_ref[f0:f0 + tf, :])
        o_ref[...] = acc

    @pl.when(i >= used_ref[0])
    def _():
        o_ref[...] = jnp.zeros(o_ref.shape, F32)


def _experts(grouped, gain, w1, w3, w2, tile_expert, used, *, tile, tf):
    p, d = grouped.shape
    dff = w1.shape[2]
    row_tile = lambda i, te, used: (jnp.minimum(i, used[0] - 1), 0)
    expert = lambda i, te, used: (te[i], 0, 0)
    return pl.pallas_call(
        functools.partial(_experts_kernel, tf=tf),
        grid_spec=pltpu.PrefetchScalarGridSpec(
            num_scalar_prefetch=2,
            grid=(p // tile,),
            in_specs=[pl.BlockSpec((tile, d), row_tile),
                      pl.BlockSpec((1, d), lambda i, te, used: (0, 0)),
                      pl.BlockSpec((None, d, dff), expert),
                      pl.BlockSpec((None, d, dff), expert),
                      pl.BlockSpec((None, dff, d), expert)],
            out_specs=pl.BlockSpec((tile, d), lambda i, te, used: (i, 0))),
        out_shape=jax.ShapeDtypeStruct((p, d), F32),
        compiler_params=pltpu.CompilerParams(dimension_semantics=("arbitrary",),
                                             vmem_limit_bytes=MOE_VMEM_LIMIT_BYTES),
        name="experts",
    )(tile_expert, used, grouped, gain, w1, w3, w2)


def _combine_kernel(*refs, final_norm):
    if final_norm:
        dest_ref, x_ref, route_ref, fin_ref, y_ref, o_ref, ya_s, yb_s, sem = refs
    else:
        dest_ref, x_ref, route_ref, y_ref, o_ref, ya_s, yb_s, sem = refs
    bufs = (ya_s, yb_s)
    src_at = lambda r, s, d: y_ref.at[pl.ds(d, 1), :]
    dst_at = lambda r, s, d: bufs[s].at[pl.ds(r, 1), :]
    _for_each_row_copy(dest_ref, src_at, dst_at, sem, x_ref.shape[0], _start_copy)
    _for_each_row_copy(dest_ref, src_at, dst_at, sem, x_ref.shape[0], _wait_copy)
    route = route_ref[...]
    out = (x_ref[...] + route[:, ROUTE_WEIGHT:ROUTE_WEIGHT + 1] * ya_s[...]
           + route[:, ROUTE_WEIGHT + 1:ROUTE_WEIGHT + 2] * yb_s[...])
    if final_norm:
        out = _rms(out, fin_ref[...])
    o_ref[...] = out


def _combine(x, route, dest3, y, *, tm, final_gain=None):
    n, d = x.shape
    final_norm = final_gain is not None
    in_specs = [pl.BlockSpec((None, 1, 2 * tm), lambda i: (i, 0, 0), memory_space=pltpu.SMEM),
                pl.BlockSpec((tm, d), lambda i: (i, 0)),
                pl.BlockSpec((tm, LANES), lambda i: (i, 0))]
    args = [dest3, x, route]
    if final_norm:
        in_specs.append(pl.BlockSpec((1, d), lambda i: (0, 0)))
        args.append(final_gain)
    in_specs.append(pl.BlockSpec(memory_space=pl.ANY))
    args.append(y)
    return pl.pallas_call(
        functools.partial(_combine_kernel, final_norm=final_norm),
        grid=(n // tm,),
        in_specs=in_specs,
        out_specs=pl.BlockSpec((tm, d), lambda i: (i, 0)),
        out_shape=jax.ShapeDtypeStruct((n, d), F32),
        scratch_shapes=[pltpu.VMEM((tm, d), F32), pltpu.VMEM((tm, d), F32), pltpu.SemaphoreType.DMA(())],
        compiler_params=pltpu.CompilerParams(dimension_semantics=("arbitrary",)),
        name="combine",
    )(*args)


def _moe(x, gain, router_w, w1, w3, w2, *, tile, final_gain=None):
    n, d = x.shape
    max_tiles = 2 * n // tile + N_EXPERTS
    route, counts = _router(x, gain, router_w, tm=tile)
    counts = counts[0, :N_EXPERTS].astype(jnp.int32)
    tiles_per = (counts + tile - 1) // tile
    tile_end = jnp.cumsum(tiles_per)
    group_row0 = (tile_end - tiles_per) * tile
    expert = route[:, ROUTE_EXPERT:ROUTE_EXPERT + 2].astype(jnp.int32)
    rank = route[:, ROUTE_RANK:ROUTE_RANK + 2].astype(jnp.int32)
    dest3 = (group_row0[expert] + rank).reshape(n // tile, 1, 2 * tile)
    used = tile_end[-1:]
    tiles = jnp.minimum(jnp.arange(max_tiles, dtype=jnp.int32), used[0] - 1)
    tile_expert = jnp.sum(tiles[:, None] >= tile_end[None, :], axis=1).astype(jnp.int32)
    grouped = _dispatch(x, dest3, jnp.zeros((max_tiles * tile, d), F32), tm=tile)
    y = _experts(grouped, gain, w1, w3, w2, tile_expert, used, tile=tile, tf=w1.shape[2] // 2)
    return _combine(x, route, dest3, y, tm=tile, final_gain=final_gain)


def _pad_cols(w, width):
    return jnp.pad(w, ((0, 0), (0, width - w.shape[1])))


def _layout_w_in(w):
    qk = GDN_HEADS * GDN_D
    o = 0
    parts = {}
    for name, width in (("q", qk), ("k", qk), ("v", qk), ("z", qk), ("a", GDN_HEADS), ("b", GDN_HEADS),
                        ("cq", Q_LORA), ("ckv", KV_LORA), ("kr", MLA_ROPE), ("gate", 2 * w.shape[0])):
        parts[name] = w[:, o:o + width]
        o += width
    small = jnp.concatenate(
        [_pad_cols(jnp.concatenate([parts["a"], parts["b"]], axis=1), ROPE_LANE0),
         _pad_cols(parts["kr"], LANES - ROPE_LANE0)], axis=1)
    return jnp.concatenate([parts["gate"], parts["q"], parts["k"], parts["v"], parts["z"], parts["cq"],
                            small, parts["ckv"]], axis=1).astype(BF16)


def _layout_heads(w, widths, pick, pad_to):
    per = sum(widths)
    k = w.shape[0]
    w = w.reshape(k, MLA_HEADS, per)
    start = sum(widths[:pick[0]])
    stop = sum(widths[:pick[1]])
    seg = w[:, :, start:stop]
    seg = jnp.pad(seg, ((0, 0), (0, 0), (0, pad_to - (stop - start))))
    return seg.reshape(k, MLA_HEADS * pad_to).astype(BF16)


def kernel(x, positions, mix_norm, w_in, gate_bias, conv_w, a_log, dt_bias, gdn_out_norm, w_gdn_o,
           q_a_norm, w_q_b, kv_a_norm, w_kv_b, w_mla_o, w_out, ffn_norm, dense_w1, dense_w3, dense_w2,
           router_w, moe_w1, moe_w3, moe_w2, final_norm):
    b, s, d = x.shape
    n = b * s
    depth = w_in.shape[0]
    xf = x.reshape(n, d)
    pos = positions.astype(F32).reshape(n, 1)
    inv_freq = 1.0 / (ROPE_THETA ** (jnp.arange(0, MLA_ROPE, 2, dtype=F32) / MLA_ROPE))
    invf = jnp.zeros((1, LANES), F32).at[0, ROPE_LANE0:ROPE_LANE0 + MLA_ROPE].set(
        jnp.concatenate([inv_freq, inv_freq]))
    row = lambda v: v.reshape(1, -1)

    for l in range(depth):
        proj = _inproj(xf, row(mix_norm[l]), _layout_w_in(w_in[l]), tm=1024, tn=IN_COLS_PAD // 2)
        o_a = _gdn(proj.reshape(b, s, IN_COLS_PAD), conv_w[l], _pad_cols(row(a_log[l]), LANES),
                   _pad_cols(row(dt_bias[l]), LANES), row(gdn_out_norm[l]))
        wq = _layout_heads(w_q_b[l], (MLA_NOPE, MLA_ROPE), (0, 2), LANES)
        wk = _layout_heads(w_kv_b[l], (MLA_NOPE, MLA_V), (0, 1), LANES)
        wv = _layout_heads(w_kv_b[l], (MLA_NOPE, MLA_V), (1, 2), MLA_V)
        q, k, v = _mla_prep(proj, pos, invf, row(q_a_norm[l]), row(kv_a_norm[l]), wq, wk, wv, tm=512)
        o_b = _attention(q.reshape(b, s, -1), k.reshape(b, s, -1), v.reshape(b, s, -1), tq=512, tk=512)
        xf = _merge(xf, o_a.reshape(n, -1), o_b.reshape(n, -1), proj, row(gate_bias[l]),
                    w_gdn_o[l].astype(BF16), w_mla_o[l].astype(BF16), w_out[l].astype(BF16), tm=512)
        fin = row(final_norm) if l == depth - 1 else None
        j = l // 2
        if l % 2 == 0:
            xf = _ffn(xf, row(ffn_norm[l]), dense_w1[j].astype(BF16), dense_w3[j].astype(BF16),
                      dense_w2[j].astype(BF16), tm=1024, tf=1792, final_gain=fin)
        else:
            xf = _moe(xf, row(ffn_norm[l]), _pad_cols(router_w[j], LANES), moe_w1[j].astype(BF16),
                      moe_w3[j].astype(BF16), moe_w2[j].astype(BF16), tile=MOE_TILE, final_gain=fin)
    return xf.reshape(b, s, d)
```

```python
import functools
import math

import jax
import jax.numpy as jnp
from jax import lax
from jax.experimental import pallas as pl
from jax.experimental.pallas import tpu as pltpu

F32 = jnp.float32
BF16 = jnp.bfloat16
HIGHEST = lax.Precision.HIGHEST

LANES = 128
SUBLANES = 8
MXU_DIM = 256
EPS = 1e-6
LOG2_E = math.log2(math.e)
CHUNK = 64
GDN_HEADS = 4
GDN_D = 128
GDN_BLOCK = 128
GDN_TILE = 512
GDN_SOLVE_BLOCKS = 4
CONV_K = 4
MLA_HEADS = 8
MLA_NOPE = 64
MLA_ROPE = 32
MLA_V = 64
Q_LORA = 384
KV_LORA = 256
ROPE_THETA = 10000.0
N_EXPERTS = 8
MOE_TILE = 512
ROW_COPY_UNROLL = 8
MOE_VMEM_LIMIT_BYTES = 56 * 1024 * 1024

COL_GATE = 0
COL_Q = 16
COL_K = 20
COL_V = 24
COL_Z = 28
COL_CQ = 32
COL_SMALL = 35
COL_CKV = 36
IN_COLS_PAD = 38 * LANES
ROPE_LANE0 = 64
ROPE_PACK = LANES // MLA_ROPE
MLA_PREP_TILE = 512


def _dot(a, b, precision=None):
    return jnp.dot(a, b, preferred_element_type=F32, precision=precision)


def _dot_nt(a, b, precision=None):
    return lax.dot_general(a, b, (((1,), (1,)), ((), ())), preferred_element_type=F32,
                           precision=precision)


def _sigmoid(x):
    return 1.0 / (1.0 + jnp.exp(-x))


def _silu(x):
    return x * _sigmoid(x)


def _rms(x, gain):
    return x * lax.rsqrt(jnp.mean(x * x, axis=-1, keepdims=True) + EPS) * gain


def _mm(a, b, nt=False):
    dot = _dot_nt if nt else _dot
    return dot(a.astype(BF16), b.astype(BF16))


def _inproj_kernel(x_ref, g_ref, w_ref, o_ref):
    h = _rms(x_ref[...], g_ref[...]).astype(BF16)
    o_ref[...] = _dot(h, w_ref[...])


def _inproj(x, gain, w, *, tm, tn):
    n, d = x.shape
    m = w.shape[1]
    return pl.pallas_call(
        _inproj_kernel,
        grid=(m // tn, n // tm),
        in_specs=[pl.BlockSpec((tm, d), lambda j, i: (i, 0)),
                  pl.BlockSpec((1, d), lambda j, i: (0, 0)),
                  pl.BlockSpec((d, tn), lambda j, i: (0, j))],
        out_specs=pl.BlockSpec((tm, tn), lambda j, i: (i, j)),
        out_shape=jax.ShapeDtypeStruct((n, m), F32),
        compiler_params=pltpu.CompilerParams(dimension_semantics=("arbitrary", "arbitrary")),
        name="inproj",
    )(x, gain, w)


def _gdn_kernel(q_ref, k_ref, v_ref, z_ref, sm_ref, cw_ref, alog_ref, dtb_ref, onorm_ref, o_ref,
                state_s, halo_s, mask_s, u_s, w_s, qd_s, kdt_s, at_s, cd_s, *, tile):
    blk = GDN_BLOCK
    nblk = tile // blk
    qk = GDN_HEADS * GDN_D
    n_levels = blk.bit_length() - 1
    row = lax.broadcasted_iota(jnp.int32, (blk, blk), 0)
    col = lax.broadcasted_iota(jnp.int32, (blk, blk), 1)

    @pl.when((pl.program_id(0) == 0) & (pl.program_id(1) == 0))
    def _():
        for lvl in range(n_levels):
            pair = (row >> (lvl + 1)) == (col >> (lvl + 1))
            mask_s[lvl] = (pair & (((row >> lvl) & 1) == 1) & (((col >> lvl) & 1) == 0)).astype(F32)

    @pl.when(pl.program_id(1) == 0)
    def _():
        state_s[...] = jnp.zeros_like(state_s)
        halo_s[...] = jnp.zeros_like(halo_s)

    lane_row = lax.broadcasted_iota(jnp.int32, (1, LANES), 1)
    neg_a = jnp.where(lane_row < GDN_HEADS, -jnp.exp(alog_ref[...]), 0.0)
    dtb = dtb_ref[...]
    srcs = (q_ref, k_ref, v_ref)

    def l2n(x):
        return x * lax.rsqrt(jnp.sum(x * x, axis=-1, keepdims=True) + EPS)

    def solve_phase(c2, carry):
        incl = row >= col
        strict = row > col
        eye_f = (row == col).astype(F32)
        tril = incl.astype(F32).astype(BF16)
        chains = [(b, h) for b in range(GDN_SOLVE_BLOCKS) for h in range(GDN_HEADS)]
        rows, gcum, gcum_t, e_g, e_rest, beta_all, conv_silu = {}, {}, {}, {}, {}, {}, {}
        for b in range(GDN_SOLVE_BLOCKS):
            c = c2 * GDN_SOLVE_BLOCKS + b
            r0 = pl.multiple_of(c * blk, blk)
            p0 = pl.multiple_of(jnp.maximum(r0 - SUBLANES, 0), SUBLANES)
            rows[b] = pl.ds(r0, blk)
            sm = sm_ref[rows[b], :]
            xa = sm + dtb
            g = neg_a * (jnp.maximum(xa, 0.0) + jnp.log(1.0 + jnp.exp(-jnp.abs(xa))))
            beta_all[b] = _sigmoid(sm)
            g_hi = g.astype(BF16)
            g_r = g - g_hi.astype(F32)
            g_mid = g_r.astype(BF16)
            g_lo = (g_r - g_mid.astype(F32)).astype(BF16)
            gcum[b] = _dot(tril, g_hi) + (_dot(tril, g_mid) + _dot(tril, g_lo))
            gcum_t[b] = gcum[b].T
            g_last = gcum[b][blk - 1:blk, :]
            e_g[b] = jnp.exp(gcum[b])
            e_rest[b] = jnp.exp(g_last - gcum[b])
            cd_s[pl.ds(c, 1), :] = jnp.exp(g_last)

            def conv_silu_b(i, h, first=c == 0, p0=p0, rows_b=rows[b]):
                hs = slice(h * GDN_D, (h + 1) * GDN_D)
                cs = slice(i * qk + h * GDN_D, i * qk + (h + 1) * GDN_D)
                cur = srcs[i][rows_b, hs]
                prev = jnp.where(first, halo_s[:, cs], srcs[i][pl.ds(p0, SUBLANES), hs])
                ext = jnp.concatenate([prev, cur], axis=0)
                cw = cw_ref[:, cs]
                y = cur * cw[CONV_K - 1:CONV_K, :]
                for s in range(1, CONV_K):
                    y = y + pltpu.roll(ext, s, 0)[SUBLANES:, :] * cw[CONV_K - 1 - s:CONV_K - s, :]
                return _silu(y)

            conv_silu[b] = conv_silu_b

        k = {ch: l2n(conv_silu[ch[0]](1, ch[1])) for ch in chains}
        k_b = {ch: k[ch].astype(BF16) for ch in chains}
        beta = {(b, h): beta_all[b][:, GDN_HEADS + h:GDN_HEADS + h + 1] for b, h in chains}
        kb = {ch: k[ch] * beta[ch] for ch in chains}
        decay = {}
        for b, h in chains:
            gc = jnp.broadcast_to(gcum[b][:, h:h + 1], (blk, blk))
            gr = jnp.broadcast_to(gcum_t[b][h:h + 1, :], (blk, blk))
            decay[b, h] = jnp.exp(jnp.where(incl, gc - gr, -jnp.inf))
        lower = {ch: jnp.where(strict, _dot_nt(kb[ch].astype(BF16), k_b[ch]) * decay[ch], 0.0)
                 for ch in chains}
        mask = mask_s[0]
        inv = {ch: eye_f - lower[ch] * mask for ch in chains}
        for lvl in range(1, n_levels):
            mask = mask_s[lvl]
            inv_b = {ch: inv[ch].astype(BF16) for ch in chains}
            half = {ch: _mm(inv_b[ch], lower[ch] * mask).astype(BF16) for ch in chains}
            inv = {ch: inv[ch] - _dot(half[ch], inv_b[ch]) for ch in chains}
        inv_b = {ch: inv[ch].astype(BF16) for ch in chains}
        for b, h in chains:
            ch = (b, h)
            eg = e_g[b][:, h:h + 1]
            q = l2n(conv_silu[b](0, h)) * (GDN_D ** -0.5)
            u_s[h, rows[b], :] = _mm(inv_b[ch], conv_silu[b](2, h) * beta[ch])
            w_s[h, rows[b], :] = _mm(inv_b[ch], kb[ch] * eg).astype(BF16)
            qd_s[h, rows[b], :] = (q * eg).astype(BF16)
            kdt_s[h, rows[b], :] = (k[ch] * e_rest[b][:, h:h + 1]).T.astype(BF16)
            at_s[h, rows[b], :] = (_dot_nt(q.astype(BF16), k_b[ch]) * decay[ch]).astype(BF16)
        return carry

    lax.fori_loop(0, nblk // GDN_SOLVE_BLOCKS, solve_phase, 0)

    onorm = onorm_ref[...]

    def scan_phase(c, carry):
        r0 = pl.multiple_of(c * blk, blk)
        rows = pl.ds(r0, blk)
        cd = cd_s[pl.ds(c, 1), :]
        heads = range(GDN_HEADS)
        state = [state_s[h] for h in heads]
        state_b = [state[h].astype(BF16) for h in heads]
        v_new = [u_s[h, rows, :] - _dot(w_s[h, rows, :], state_b[h]) for h in heads]
        v_new_b = [v_new[h].astype(BF16) for h in heads]
        for h in heads:
            state_s[h] = state[h] * cd[:, h:h + 1] + _dot(kdt_s[h, rows, :], v_new_b[h])
        for h in heads:
            hs = slice(h * GDN_D, (h + 1) * GDN_D)
            o = _dot(qd_s[h, rows, :], state_b[h]) + _dot(at_s[h, rows, :], v_new_b[h])
            o = _rms(o, onorm) * _silu(z_ref[rows, hs])
            o_ref[rows, hs] = o.astype(o_ref.dtype)
        return carry

    lax.fori_loop(0, nblk, scan_phase, 0)

    for i in range(3):
        halo_s[:, i * qk:(i + 1) * qk] = srcs[i][tile - SUBLANES:tile, :]


def _gdn(proj3, conv_w, a_log_row, dt_bias_row, out_norm):
    b, s, _ = proj3.shape
    tile = min(GDN_TILE, s)
    qk = GDN_HEADS * GDN_D
    heads_blk = qk // LANES

    def cols(c0):
        return pl.BlockSpec((None, tile, qk), lambda i, t: (i, t, c0 // heads_blk))

    row = lambda width: pl.BlockSpec((1, width), lambda i, t: (0, 0))
    per_head = lambda dtype: pltpu.VMEM((GDN_HEADS, tile, GDN_D), dtype)
    return pl.pallas_call(
        functools.partial(_gdn_kernel, tile=tile),
        grid=(b, s // tile),
        in_specs=[cols(COL_Q), cols(COL_K), cols(COL_V), cols(COL_Z),
                  pl.BlockSpec((None, tile, LANES), lambda i, t: (i, t, COL_SMALL)),
                  pl.BlockSpec((CONV_K, 3 * qk), lambda i, t: (0, 0)),
                  row(LANES), row(LANES), row(GDN_D)],
        out_specs=pl.BlockSpec((None, tile, qk), lambda i, t: (i, t, 0)),
        out_shape=jax.ShapeDtypeStruct((b, s, qk), BF16),
        scratch_shapes=[pltpu.VMEM((GDN_HEADS, GDN_D, GDN_D), F32),
                        pltpu.VMEM((SUBLANES, 3 * qk), F32),
                        pltpu.VMEM((GDN_BLOCK.bit_length() - 1, GDN_BLOCK, GDN_BLOCK), F32),
                        per_head(F32), per_head(BF16), per_head(BF16), per_head(BF16), per_head(BF16),
                        pltpu.VMEM((max(tile // GDN_BLOCK, SUBLANES), LANES), F32)],
        compiler_params=pltpu.CompilerParams(dimension_semantics=("arbitrary", "arbitrary")),
        name="gdn",
    )(proj3, proj3, proj3, proj3, proj3, conv_w, a_log_row, dt_bias_row, out_norm)


def _mla_prep_kernel(cqa_ref, ckv_ref, pos_ref, invf_ref, qn_ref, kvn_ref, wq_ref, wqr_ref, wk_ref, wv_ref,
                     q_ref, k_ref, v_ref):
    cqa = cqa_ref[...]
    c_q = cqa[:, :Q_LORA]
    small = cqa[:, Q_LORA:]
    lane = lax.broadcasted_iota(jnp.int32, small.shape, 1)
    ang = pos_ref[...] * invf_ref[...]
    cos_p = jnp.cos(ang)
    sin_p = jnp.sin(ang)
    unpack = lambda t: jnp.concatenate(
        [pltpu.roll(t, (ROPE_LANE0 - g * MLA_ROPE) % LANES, 1) for g in range(ROPE_PACK)], axis=0)
    cosv = unpack(cos_p)
    sinv = unpack(sin_p)
    half = MLA_ROPE // 2
    lo = (lane >= ROPE_LANE0) & (lane < ROPE_LANE0 + half)
    hi = (lane >= ROPE_LANE0 + half) & (lane < ROPE_LANE0 + MLA_ROPE)
    sin_lo = jnp.where(lo, -sinv, 0.0)
    sin_hi = jnp.where(hi, sinv, 0.0)

    def rotate(t, cos_t):
        return t * cos_t + (pltpu.roll(t, LANES - half, 1) * sin_lo + pltpu.roll(t, half, 1) * sin_hi)

    scale = (MLA_NOPE + MLA_ROPE) ** -0.5 * LOG2_E
    hq = _rms(c_q, qn_ref[...]).astype(BF16)
    q = _dot(hq, wq_ref[...])
    q_swapped = _dot(hq, wqr_ref[...])
    cos_q = jnp.where(lo | hi, cosv, 1.0) * scale
    sin_q = jnp.where(lo | hi, sinv, 0.0) * scale
    hkv = _rms(ckv_ref[...], kvn_ref[...]).astype(BF16)
    k_nope = _dot(hkv, wk_ref[...])
    k_pe = rotate(small, jnp.where(lo | hi, cosv, 0.0))
    for h in range(MLA_HEADS):
        sl = slice(h * LANES, (h + 1) * LANES)
        q_ref[:, sl] = (q[:, sl] * cos_q + q_swapped[:, sl] * sin_q).astype(q_ref.dtype)
        k_ref[:, sl] = (k_nope[:, sl] + k_pe).astype(k_ref.dtype)
    v_ref[...] = _dot(hkv, wv_ref[...]).astype(v_ref.dtype)


def _mla_prep(proj, pos, invf, q_a_norm, kv_a_norm, wq, wq_swapped, wk, wv, *, tm):
    n = proj.shape[0]
    hq = MLA_HEADS * LANES
    hv = MLA_HEADS * MLA_V
    full = lambda shape: pl.BlockSpec(shape, lambda i: (0, 0))
    return pl.pallas_call(
        _mla_prep_kernel,
        grid=(n // tm,),
        in_specs=[pl.BlockSpec((tm, Q_LORA + LANES), lambda i: (i, COL_CQ * LANES // (Q_LORA + LANES))),
                  pl.BlockSpec((tm, KV_LORA), lambda i: (i, COL_CKV * LANES // KV_LORA)),
                  pl.BlockSpec((tm // ROPE_PACK, LANES), lambda i: (i, 0)),
                  full((1, LANES)), full((1, Q_LORA)), full((1, KV_LORA)),
                  full((Q_LORA, hq)), full((Q_LORA, hq)), full((KV_LORA, hq)), full((KV_LORA, hv))],
        out_specs=[pl.BlockSpec((tm, hq), lambda i: (i, 0)),
                   pl.BlockSpec((tm, hq), lambda i: (i, 0)),
                   pl.BlockSpec((tm, hv), lambda i: (i, 0))],
        out_shape=[jax.ShapeDtypeStruct((n, hq), BF16),
                   jax.ShapeDtypeStruct((n, hq), BF16),
                   jax.ShapeDtypeStruct((n, hv), BF16)],
        compiler_params=pltpu.CompilerParams(dimension_semantics=("arbitrary",)),
        name="mla_prep",
    )(proj, proj, pos, invf, q_a_norm, kv_a_norm, wq, wq_swapped, wk, wv)


def _attn_kernel(q_ref, k_ref, v_ref, o_ref, s_s, m_s, l_s, acc_s, *, tq, tk):
    qi = pl.program_id(2)
    nsub = tq // tk
    lane_tiles = range(tk // LANES)
    heads = range(2)
    r_chunk = lax.broadcasted_iota(jnp.int32, (tq, tk), 0) // CHUNK
    c_chunk = lax.broadcasted_iota(jnp.int32, (tq, tk), 1) // CHUNK
    q = [q_ref[:, h * LANES:(h + 1) * LANES] for h in heads]
    m_s[...] = jnp.full(m_s.shape, -jnp.inf, F32)

    def scores(j, visible):
        k0 = pl.multiple_of(j * tk, tk)
        for h in heads:
            s = _dot_nt(q[h], k_ref[pl.ds(k0, tk), h * LANES:(h + 1) * LANES])
            if visible is not None:
                s = jnp.where(visible, s, -jnp.inf)
            s_s[h, j] = s
            part = m_s[h]
            for c in lane_tiles:
                part = jnp.maximum(part, s[:, c * LANES:(c + 1) * LANES])
            m_s[h] = part

    def scores_body(j, carry):
        scores(j, None)
        return carry

    lax.fori_loop(0, qi * nsub, scores_body, 0)
    for d in range(nsub):
        scores(qi * nsub + d, c_chunk + d * (tk // CHUNK) <= r_chunk)
    for h in heads:
        m_s[h] = jnp.broadcast_to(jnp.max(m_s[h], axis=-1, keepdims=True), (tq, LANES))
    l_s[...] = jnp.zeros(l_s.shape, F32)
    acc_s[...] = jnp.zeros(acc_s.shape, F32)

    def accumulate(j, carry):
        k0 = pl.multiple_of(j * tk, tk)
        v = v_ref[pl.ds(k0, tk), :]
        for h in heads:
            m = m_s[h]
            p = [jnp.exp2(s_s[h, j, :, c * LANES:(c + 1) * LANES] - m) for c in lane_tiles]
            l_s[h] += functools.reduce(lambda a, b: a + b, p)
            acc_s[h] += _dot(jnp.concatenate(p, axis=1).astype(BF16), v)
        return carry

    lax.fori_loop(0, (qi + 1) * nsub, accumulate, 0)
    outs = [acc_s[h] * (1.0 / jnp.sum(l_s[h], axis=-1, keepdims=True)) for h in heads]
    lane = lax.broadcasted_iota(jnp.int32, (tq, LANES), 1)
    o_ref[...] = jnp.where(lane < MLA_V, outs[0], outs[1]).astype(o_ref.dtype)


def _attention(q3, k3, v3, *, tq, tk):
    b, s, _ = q3.shape
    row_stat = pltpu.VMEM((2, tq, LANES), F32)
    return pl.pallas_call(
        functools.partial(_attn_kernel, tq=tq, tk=tk),
        scratch_shapes=[pltpu.VMEM((2, s // tk, tq, tk), F32), row_stat, row_stat, row_stat],
        grid=(b, MLA_HEADS // 2, s // tq),
        in_specs=[pl.BlockSpec((None, tq, 2 * LANES), lambda i, p, j: (i, j, p)),
                  pl.BlockSpec((None, s, 2 * LANES), lambda i, p, j: (i, 0, p)),
                  pl.BlockSpec((None, s, 2 * MLA_V), lambda i, p, j: (i, 0, p))],
        out_specs=pl.BlockSpec((None, tq, 2 * MLA_V), lambda i, p, j: (i, j, p)),
        out_shape=jax.ShapeDtypeStruct((b, s, MLA_HEADS * MLA_V), BF16),
        compiler_params=pltpu.CompilerParams(dimension_semantics=("arbitrary",) * 3),
        name="attention",
    )(q3, k3, v3)


def _merge_kernel(x_ref, oa_ref, ob_ref, gate_ref, gb_ref, wa_ref, wb_ref, wo_ref, o_ref):
    d = x_ref.shape[1]
    g = _sigmoid(gate_ref[...] + gb_ref[...])
    merged = g[:, :d] * _dot(oa_ref[...], wa_ref[...]) + g[:, d:] * _dot(ob_ref[...], wb_ref[...])
    o_ref[...] = x_ref[...] + _dot(merged.astype(BF16), wo_ref[...])


def _merge(x, o_a, o_b, proj, gate_bias, w_a, w_b, w_o, *, tm):
    n, d = x.shape
    full = lambda a: pl.BlockSpec(a.shape, lambda i: (0, 0))
    return pl.pallas_call(
        _merge_kernel,
        grid=(n // tm,),
        in_specs=[pl.BlockSpec((tm, d), lambda i: (i, 0)),
                  pl.BlockSpec((tm, o_a.shape[1]), lambda i: (i, 0)),
                  pl.BlockSpec((tm, o_b.shape[1]), lambda i: (i, 0)),
                  pl.BlockSpec((tm, 2 * d), lambda i: (i, COL_GATE)),
                  full(gate_bias), full(w_a), full(w_b), full(w_o)],
        out_specs=pl.BlockSpec((tm, d), lambda i: (i, 0)),
        out_shape=jax.ShapeDtypeStruct((n, d), F32),
        compiler_params=pltpu.CompilerParams(dimension_semantics=("arbitrary",)),
        name="merge",
    )(x, o_a, o_b, proj, gate_bias, w_a, w_b, w_o)


def _ffn_kernel(*refs, final_norm, sub):
    if final_norm:
        x_ref, g_ref, w1_ref, w3_ref, w2_ref, fin_ref, o_ref, h_s, hid_s, acc_s = refs
    else:
        x_ref, g_ref, w1_ref, w3_ref, w2_ref, o_ref, h_s, hid_s, acc_s = refs
    f = pl.program_id(1)

    @pl.when(f == 0)
    def _():
        x = x_ref[...]
        h_s[...] = _rms(x, g_ref[...]).astype(BF16)
        acc_s[...] = x

    h = h_s[...]
    for c0 in range(0, hid_s.shape[1], sub):
        cols = slice(c0, c0 + sub)
        hid_s[:, cols] = (_silu(_dot(h, w1_ref[:, cols])) * _dot(h, w3_ref[:, cols])).astype(BF16)
    acc_s[...] += _dot(hid_s[...], w2_ref[...])

    @pl.when(f == pl.num_programs(1) - 1)
    def _():
        out = acc_s[...]
        if final_norm:
            out = _rms(out, fin_ref[...])
        o_ref[...] = out


def _ffn(x, gain, w1, w3, w2, *, tm, tf, final_gain=None):
    n, d = x.shape
    dff = w1.shape[1]
    final_norm = final_gain is not None
    row = pl.BlockSpec((1, d), lambda i, f: (0, 0))
    in_specs = [pl.BlockSpec((tm, d), lambda i, f: (i, 0)), row,
                pl.BlockSpec((d, tf), lambda i, f: (0, f)),
                pl.BlockSpec((d, tf), lambda i, f: (0, f)),
                pl.BlockSpec((tf, d), lambda i, f: (f, 0))]
    args = [x, gain, w1, w3, w2]
    if final_norm:
        in_specs.append(row)
        args.append(final_gain)
    return pl.pallas_call(
        functools.partial(_ffn_kernel, final_norm=final_norm, sub=MXU_DIM),
        grid=(n // tm, dff // tf),
        in_specs=in_specs,
        out_specs=pl.BlockSpec((tm, d), lambda i, f: (i, 0)),
        out_shape=jax.ShapeDtypeStruct((n, d), F32),
        scratch_shapes=[pltpu.VMEM((tm, d), BF16), pltpu.VMEM((tm, tf), BF16), pltpu.VMEM((tm, d), F32)],
        compiler_params=pltpu.CompilerParams(dimension_semantics=("arbitrary", "arbitrary"),
                                             vmem_limit_bytes=MOE_VMEM_LIMIT_BYTES),
        name="ffn",
    )(*args)


ROUTE_EXPERT = 0
ROUTE_RANK = 2
ROUTE_WEIGHT = 4


def _router_kernel(x_ref, g_ref, rw_ref, route_ref, count_ref, count_s, tri_s):
    tm = x_ref.shape[0]

    @pl.when(pl.program_id(0) == 0)
    def _():
        count_s[...] = jnp.zeros_like(count_s)
        r = lax.broadcasted_iota(jnp.int32, (tm, tm), 0)
        c = lax.broadcasted_iota(jnp.int32, (tm, tm), 1)
        tri_s[...] = (r > c).astype(F32).astype(BF16)

    lane = lax.broadcasted_iota(jnp.int32, (tm, LANES), 1)
    lane_f = lane.astype(F32)
    h = _rms(x_ref[...], g_ref[...])
    logits = jnp.where(lane < N_EXPERTS, _dot(h, rw_ref[...], HIGHEST), -jnp.inf)
    m1 = jnp.max(logits, axis=-1, keepdims=True)
    i1 = jnp.min(jnp.where(logits == m1, lane_f, float(LANES)), axis=-1, keepdims=True)
    rest = jnp.where(lane_f == i1, -jnp.inf, logits)
    m2 = jnp.max(rest, axis=-1, keepdims=True)
    i2 = jnp.min(jnp.where(rest == m2, lane_f, float(LANES)), axis=-1, keepdims=True)
    t = jnp.exp(m2 - m1)
    sel1 = lane_f == i1
    sel2 = lane_f == i2
    chosen = jnp.where(sel1 | sel2, 1.0, 0.0)
    before = _dot(tri_s[...], chosen.astype(BF16)) + count_s[...]
    r1 = jnp.sum(jnp.where(sel1, before, 0.0), axis=-1, keepdims=True)
    r2 = jnp.sum(jnp.where(sel2, before, 0.0), axis=-1, keepdims=True)
    count_s[...] += jnp.sum(chosen, axis=0, keepdims=True)
    fields = (i1, i2, r1, r2, 1.0 / (1.0 + t), t / (1.0 + t))
    route = jnp.zeros((tm, LANES), F32)
    for k, val in enumerate(fields):
        route = jnp.where(lane == k, val, route)
    route_ref[...] = route
    count_ref[...] = jnp.broadcast_to(count_s[...], count_ref.shape)


def _router(x, gain, router_w, *, tm):
    n, d = x.shape
    return pl.pallas_call(
        _router_kernel,
        grid=(n // tm,),
        in_specs=[pl.BlockSpec((tm, d), lambda i: (i, 0)),
                  pl.BlockSpec((1, d), lambda i: (0, 0)),
                  pl.BlockSpec((d, LANES), lambda i: (0, 0))],
        out_specs=[pl.BlockSpec((tm, LANES), lambda i: (i, 0)),
                   pl.BlockSpec((SUBLANES, LANES), lambda i: (0, 0))],
        out_shape=[jax.ShapeDtypeStruct((n, LANES), F32),
                   jax.ShapeDtypeStruct((SUBLANES, LANES), F32)],
        scratch_shapes=[pltpu.VMEM((1, LANES), F32), pltpu.VMEM((tm, tm), BF16)],
        compiler_params=pltpu.CompilerParams(dimension_semantics=("arbitrary",)),
        name="router",
    )(x, gain, router_w)


def _start_copy(copy):
    copy.start()


def _wait_copy(copy):
    copy.wait()


def _for_each_row_copy(dest_ref, src_at, dst_at, sem, n_rows, action):
    def body(r, carry):
        for s in range(2):
            d = dest_ref[0, 2 * r + s]
            action(pltpu.make_async_copy(src_at(r, s, d), dst_at(r, s, d), sem))
        return carry

    lax.fori_loop(0, n_rows, body, 0, unroll=ROW_COPY_UNROLL)


def _dispatch_kernel(dest_ref, x_ref, grouped_in_ref, grouped_ref, sem):
    del grouped_in_ref
    src_at = lambda r, s, d: x_ref.at[pl.ds(r, 1), :]
    dst_at = lambda r, s, d: grouped_ref.at[pl.ds(d, 1), :]
    _for_each_row_copy(dest_ref, src_at, dst_at, sem, x_ref.shape[0], _start_copy)
    _for_each_row_copy(dest_ref, src_at, dst_at, sem, x_ref.shape[0], _wait_copy)


def _dispatch(x, dest3, grouped_zeros, *, tm):
    n, d = x.shape
    return pl.pallas_call(
        _dispatch_kernel,
        grid=(n // tm,),
        in_specs=[pl.BlockSpec((None, 1, 2 * tm), lambda i: (i, 0, 0), memory_space=pltpu.SMEM),
                  pl.BlockSpec((tm, d), lambda i: (i, 0)),
                  pl.BlockSpec(memory_space=pl.ANY)],
        out_specs=pl.BlockSpec(memory_space=pl.ANY),
        out_shape=jax.ShapeDtypeStruct(grouped_zeros.shape, F32),
        scratch_shapes=[pltpu.SemaphoreType.DMA(())],
        input_output_aliases={2: 0},
        compiler_params=pltpu.CompilerParams(dimension_semantics=("arbitrary",)),
        name="dispatch",
    )(dest3, x, grouped_zeros)


def _experts_kernel(tile_expert_ref, used_ref, x_ref, g_ref, w1_ref, w3_ref, w2_ref, o_ref, *, tf):
    del tile_expert_ref
    i = pl.program_id(0)

    @pl.when(i < used_ref[0])
    def _():
        h = _rms(x_ref[...], g_ref[...]).astype(BF16)
        acc = jnp.zeros(o_ref.shape, F32)
        for f0 in range(0, w1_ref.shape[1], tf):
            hid = _silu(_dot(h, w1_ref[:, f0:f0 + tf])) * _dot(h, w3_ref[:, f0:f0 + tf])
            acc = acc + _dot(hid.astype(BF16), w2_ref[f0:f0 + tf, :])
        o_ref[...] = acc

    @pl.when(i >= used_ref[0])
    def _():
        o_ref[...] = jnp.zeros(o_ref.shape, F32)


def _experts(grouped, gain, w1, w3, w2, tile_expert, used, *, tile, tf):
    p, d = grouped.shape
    dff = w1.shape[2]
    row_tile = lambda i, te, used: (jnp.minimum(i, used[0] - 1), 0)
    expert = lambda i, te, used: (te[i], 0, 0)
    return pl.pallas_call(
        functools.partial(_experts_kernel, tf=tf),
        grid_spec=pltpu.PrefetchScalarGridSpec(
            num_scalar_prefetch=2,
            grid=(p // tile,),
            in_specs=[pl.BlockSpec((tile, d), row_tile),
                      pl.BlockSpec((1, d), lambda i, te, used: (0, 0)),
                      pl.BlockSpec((None, d, dff), expert),
                      pl.BlockSpec((None, d, dff), expert),
                      pl.BlockSpec((None, dff, d), expert)],
            out_specs=pl.BlockSpec((tile, d), lambda i, te, used: (i, 0))),
        out_shape=jax.ShapeDtypeStruct((p, d), F32),
        compiler_params=pltpu.CompilerParams(dimension_semantics=("arbitrary",),
                                             vmem_limit_bytes=MOE_VMEM_LIMIT_BYTES),
        name="experts",
    )(tile_expert, used, grouped, gain, w1, w3, w2)


def _combine_kernel(*refs, final_norm):
    if final_norm:
        dest_ref, x_ref, route_ref, fin_ref, y_ref, o_ref, ya_s, yb_s, sem = refs
    else:
        dest_ref, x_ref, route_ref, y_ref, o_ref, ya_s, yb_s, sem = refs
    bufs = (ya_s, yb_s)
    src_at = lambda r, s, d: y_ref.at[pl.ds(d, 1), :]
    dst_at = lambda r, s, d: bufs[s].at[pl.ds(r, 1), :]
    _for_each_row_copy(dest_ref, src_at, dst_at, sem, x_ref.shape[0], _start_copy)
    _for_each_row_copy(dest_ref, src_at, dst_at, sem, x_ref.shape[0], _wait_copy)
    route = route_ref[...]
    out = (x_ref[...] + route[:, ROUTE_WEIGHT:ROUTE_WEIGHT + 1] * ya_s[...]
           + route[:, ROUTE_WEIGHT + 1:ROUTE_WEIGHT + 2] * yb_s[...])
    if final_norm:
        out = _rms(out, fin_ref[...])
    o_ref[...] = out


def _combine(x, route, dest3, y, *, tm, final_gain=None):
    n, d = x.shape
    final_norm = final_gain is not None
    in_specs = [pl.BlockSpec((None, 1, 2 * tm), lambda i: (i, 0, 0), memory_space=pltpu.SMEM),
                pl.BlockSpec((tm, d), lambda i: (i, 0)),
                pl.BlockSpec((tm, LANES), lambda i: (i, 0))]
    args = [dest3, x, route]
    if final_norm:
        in_specs.append(pl.BlockSpec((1, d), lambda i: (0, 0)))
        args.append(final_gain)
    in_specs.append(pl.BlockSpec(memory_space=pl.ANY))
    args.append(y)
    return pl.pallas_call(
        functools.partial(_combine_kernel, final_norm=final_norm),
        grid=(n // tm,),
        in_specs=in_specs,
        out_specs=pl.BlockSpec((tm, d), lambda i: (i, 0)),
        out_shape=jax.ShapeDtypeStruct((n, d), F32),
        scratch_shapes=[pltpu.VMEM((tm, d), F32), pltpu.VMEM((tm, d), F32), pltpu.SemaphoreType.DMA(())],
        compiler_params=pltpu.CompilerParams(dimension_semantics=("arbitrary",)),
        name="combine",
    )(*args)


def _moe(x, gain, router_w, w1, w3, w2, *, tile, final_gain=None):
    n, d = x.shape
    max_tiles = 2 * n // tile + N_EXPERTS
    route, counts = _router(x, gain, router_w, tm=tile)
    counts = counts[0, :N_EXPERTS].astype(jnp.int32)
    tiles_per = (counts + tile - 1) // tile
    tile_end = jnp.cumsum(tiles_per)
    group_row0 = (tile_end - tiles_per) * tile
    expert = route[:, ROUTE_EXPERT:ROUTE_EXPERT + 2].astype(jnp.int32)
    rank = route[:, ROUTE_RANK:ROUTE_RANK + 2].astype(jnp.int32)
    dest3 = (group_row0[expert] + rank).reshape(n // tile, 1, 2 * tile)
    used = tile_end[-1:]
    tiles = jnp.minimum(jnp.arange(max_tiles, dtype=jnp.int32), used[0] - 1)
    tile_expert = jnp.sum(tiles[:, None] >= tile_end[None, :], axis=1).astype(jnp.int32)
    grouped = _dispatch(x, dest3, jnp.zeros((max_tiles * tile, d), F32), tm=tile)
    y = _experts(grouped, gain, w1, w3, w2, tile_expert, used, tile=tile, tf=w1.shape[2] // 2)
    return _combine(x, route, dest3, y, tm=tile, final_gain=final_gain)


def _pack_positions(pos, tm):
    n = pos.shape[0]
    p = pos.astype(F32).reshape(n // tm, ROPE_PACK, tm // ROPE_PACK).transpose(0, 2, 1)
    return jnp.repeat(p, MLA_ROPE, axis=2).reshape(n // ROPE_PACK, LANES)


def _pad_cols(w, width):
    return jnp.pad(w, ((0, 0), (0, width - w.shape[1])))


def _layout_w_in(w):
    qk = GDN_HEADS * GDN_D
    o = 0
    parts = {}
    for name, width in (("q", qk), ("k", qk), ("v", qk), ("z", qk), ("a", GDN_HEADS), ("b", GDN_HEADS),
                        ("cq", Q_LORA), ("ckv", KV_LORA), ("kr", MLA_ROPE), ("gate", 2 * w.shape[0])):
        parts[name] = w[:, o:o + width]
        o += width
    small = jnp.concatenate(
        [_pad_cols(jnp.concatenate([parts["a"], parts["b"]], axis=1), ROPE_LANE0),
         _pad_cols(parts["kr"], LANES - ROPE_LANE0)], axis=1)
    return jnp.concatenate([parts["gate"], parts["q"], parts["k"], parts["v"], parts["z"], parts["cq"],
                            small, parts["ckv"]], axis=1).astype(BF16)


def _layout_heads(w, widths, pick, pad_to):
    per = sum(widths)
    k = w.shape[0]
    w = w.reshape(k, MLA_HEADS, per)
    start = sum(widths[:pick[0]])
    stop = sum(widths[:pick[1]])
    seg = w[:, :, start:stop]
    seg = jnp.pad(seg, ((0, 0), (0, 0), (0, pad_to - (stop - start))))
    return seg.reshape(k, MLA_HEADS * pad_to).astype(BF16)


def _layout_q_swapped(w):
    k = w.shape[0]
    pe = w.reshape(k, MLA_HEADS, MLA_NOPE + MLA_ROPE)[:, :, MLA_NOPE:]
    half = MLA_ROPE // 2
    seg = jnp.concatenate([-pe[:, :, half:], pe[:, :, :half]], axis=2)
    seg = jnp.pad(seg, ((0, 0), (0, 0), (ROPE_LANE0, LANES - ROPE_LANE0 - MLA_ROPE)))
    return seg.reshape(k, MLA_HEADS * LANES).astype(BF16)


def kernel(x, positions, mix_norm, w_in, gate_bias, conv_w, a_log, dt_bias, gdn_out_norm, w_gdn_o,
           q_a_norm, w_q_b, kv_a_norm, w_kv_b, w_mla_o, w_out, ffn_norm, dense_w1, dense_w3, dense_w2,
           router_w, moe_w1, moe_w3, moe_w2, final_norm):
    b, s, d = x.shape
    n = b * s
    depth = w_in.shape[0]
    xf = x.reshape(n, d)
    pos = _pack_positions(positions.reshape(n), MLA_PREP_TILE)
    inv_freq = 1.0 / (ROPE_THETA ** (jnp.arange(0, MLA_ROPE, 2, dtype=F32) / MLA_ROPE))
    invf = jnp.tile(inv_freq, 2 * ROPE_PACK).reshape(1, LANES)
    row = lambda v: v.reshape(1, -1)

    for l in range(depth):
        proj = _inproj(xf, row(mix_norm[l]), _layout_w_in(w_in[l]), tm=1024, tn=IN_COLS_PAD // 2)
        o_a = _gdn(proj.reshape(b, s, IN_COLS_PAD), conv_w[l], _pad_cols(row(a_log[l]), LANES),
                   _pad_cols(row(dt_bias[l]), LANES), row(gdn_out_norm[l]))
        wq = _layout_heads(w_q_b[l], (MLA_NOPE, MLA_ROPE), (0, 2), LANES)
        wk = _layout_heads(w_kv_b[l], (MLA_NOPE, MLA_V), (0, 1), LANES)
        wv = _layout_heads(w_kv_b[l], (MLA_NOPE, MLA_V), (1, 2), MLA_V)
        q, k, v = _mla_prep(proj, pos, invf, row(q_a_norm[l]), row(kv_a_norm[l]), wq,
                            _layout_q_swapped(w_q_b[l]), wk, wv, tm=MLA_PREP_TILE)
        o_b = _attention(q.reshape(b, s, -1), k.reshape(b, s, -1), v.reshape(b, s, -1), tq=512, tk=512)
        xf = _merge(xf, o_a.reshape(n, -1), o_b.reshape(n, -1), proj, row(gate_bias[l]),
                    w_gdn_o[l].astype(BF16), w_mla_o[l].astype(BF16), w_out[l].astype(BF16), tm=512)
        fin = row(final_norm) if l == depth - 1 else None
        j = l // 2
        if l % 2 == 0:
            xf = _ffn(xf, row(ffn_norm[l]), dense_w1[j].astype(BF16), dense_w3[j].astype(BF16),
                      dense_w2[j].astype(BF16), tm=1024, tf=1792, final_gain=fin)
        else:
            xf = _moe(xf, row(ffn_norm[l]), _pad_cols(router_w[j], LANES), moe_w1[j].astype(BF16),
                      moe_w3[j].astype(BF16), moe_w2[j].astype(BF16), tile=MOE_TILE, final_gain=fin)
    return xf.reshape(b, s, d)
```

```python
import functools
import math

import jax
import jax.numpy as jnp
from jax import lax
from jax.experimental import pallas as pl
from jax.experimental.pallas import tpu as pltpu

F32 = jnp.float32
BF16 = jnp.bfloat16
HIGHEST = lax.Precision.HIGHEST

LANES = 128
SUBLANES = 8
MXU_DIM = 256
EPS = 1e-6
LOG2_E = math.log2(math.e)
CHUNK = 64
GDN_HEADS = 4
GDN_D = 128
GDN_BLOCK = 128
GDN_TILE = 512
GDN_SOLVE_BLOCKS = 4
CONV_K = 4
MLA_HEADS = 8
MLA_NOPE = 64
MLA_ROPE = 32
MLA_V = 64
Q_LORA = 384
KV_LORA = 256
ROPE_THETA = 10000.0
N_EXPERTS = 8
MOE_TILE = 512
ROW_COPY_UNROLL = 8
MOE_VMEM_LIMIT_BYTES = 56 * 1024 * 1024

COL_GATE = 0
COL_Q = 16
COL_K = 20
COL_V = 24
COL_Z = 28
COL_CQ = 32
COL_SMALL = 35
COL_CKV = 36
IN_COLS_PAD = 38 * LANES
ROPE_LANE0 = 64
ROPE_PACK = LANES // MLA_ROPE
MLA_PREP_TILE = 512


def _dot(a, b, precision=None):
    return jnp.dot(a, b, preferred_element_type=F32, precision=precision)


def _dot_nt(a, b, precision=None):
    return lax.dot_general(a, b, (((1,), (1,)), ((), ())), preferred_element_type=F32,
                           precision=precision)


def _sigmoid(x):
    return 1.0 / (1.0 + jnp.exp(-x))


def _silu(x):
    return x * _sigmoid(x)


def _rms(x, gain):
    return x * lax.rsqrt(jnp.mean(x * x, axis=-1, keepdims=True) + EPS) * gain


def _mm(a, b, nt=False):
    dot = _dot_nt if nt else _dot
    return dot(a.astype(BF16), b.astype(BF16))


def _inproj_kernel(x_ref, g_ref, w_ref, o_ref):
    h = _rms(x_ref[...], g_ref[...]).astype(BF16)
    o_ref[...] = _dot(h, w_ref[...])


def _inproj(x, gain, w, *, tm, tn):
    n, d = x.shape
    m = w.shape[1]
    return pl.pallas_call(
        _inproj_kernel,
        grid=(m // tn, n // tm),
        in_specs=[pl.BlockSpec((tm, d), lambda j, i: (i, 0)),
                  pl.BlockSpec((1, d), lambda j, i: (0, 0)),
                  pl.BlockSpec((d, tn), lambda j, i: (0, j))],
        out_specs=pl.BlockSpec((tm, tn), lambda j, i: (i, j)),
        out_shape=jax.ShapeDtypeStruct((n, m), F32),
        compiler_params=pltpu.CompilerParams(dimension_semantics=("arbitrary", "arbitrary")),
        name="inproj",
    )(x, gain, w)


def _gdn_kernel(q_ref, k_ref, v_ref, z_ref, sm_ref, cw_ref, alog_ref, dtb_ref, onorm_ref, o_ref,
                state_s, halo_s, mask_s, u_s, w_s, qd_s, kdt_s, at_s, cd_s, *, tile):
    blk = GDN_BLOCK
    nblk = tile // blk
    qk = GDN_HEADS * GDN_D
    n_levels = blk.bit_length() - 1
    row = lax.broadcasted_iota(jnp.int32, (blk, blk), 0)
    col = lax.broadcasted_iota(jnp.int32, (blk, blk), 1)

    @pl.when((pl.program_id(0) == 0) & (pl.program_id(1) == 0))
    def _():
        for lvl in range(n_levels):
            pair = (row >> (lvl + 1)) == (col >> (lvl + 1))
            mask_s[lvl] = (pair & (((row >> lvl) & 1) == 1) & (((col >> lvl) & 1) == 0)).astype(F32)

    @pl.when(pl.program_id(1) == 0)
    def _():
        state_s[...] = jnp.zeros_like(state_s)
        halo_s[...] = jnp.zeros_like(halo_s)

    lane_row = lax.broadcasted_iota(jnp.int32, (1, LANES), 1)
    neg_a = jnp.where(lane_row < GDN_HEADS, -jnp.exp(alog_ref[...]), 0.0)
    dtb = dtb_ref[...]
    srcs = (q_ref, k_ref, v_ref)

    def l2n(x):
        return x * lax.rsqrt(jnp.sum(x * x, axis=-1, keepdims=True) + EPS)

    def solve_phase(c2, carry):
        incl = row >= col
        strict = row > col
        eye_f = (row == col).astype(F32)
        tril = incl.astype(F32).astype(BF16)
        chains = [(b, h) for b in range(GDN_SOLVE_BLOCKS) for h in range(GDN_HEADS)]
        rows, gcum, gcum_t, e_g, e_rest, beta_all, conv_silu = {}, {}, {}, {}, {}, {}, {}
        for b in range(GDN_SOLVE_BLOCKS):
            c = c2 * GDN_SOLVE_BLOCKS + b
            r0 = pl.multiple_of(c * blk, blk)
            p0 = pl.multiple_of(jnp.maximum(r0 - SUBLANES, 0), SUBLANES)
            rows[b] = pl.ds(r0, blk)
            sm = sm_ref[rows[b], :]
            xa = sm + dtb
            g = neg_a * (jnp.maximum(xa, 0.0) + jnp.log(1.0 + jnp.exp(-jnp.abs(xa))))
            beta_all[b] = _sigmoid(sm)
            g_hi = g.astype(BF16)
            g_r = g - g_hi.astype(F32)
            g_mid = g_r.astype(BF16)
            g_lo = (g_r - g_mid.astype(F32)).astype(BF16)
            gcum[b] = _dot(tril, g_hi) + (_dot(tril, g_mid) + _dot(tril, g_lo))
            gcum_t[b] = gcum[b].T
            g_last = gcum[b][blk - 1:blk, :]
            e_g[b] = jnp.exp(gcum[b])
            e_rest[b] = jnp.exp(g_last - gcum[b])
            cd_s[pl.ds(c, 1), :] = jnp.exp(g_last)

            def conv_silu_b(i, h, first=c == 0, p0=p0, rows_b=rows[b]):
                hs = slice(h * GDN_D, (h + 1) * GDN_D)
                cs = slice(i * qk + h * GDN_D, i * qk + (h + 1) * GDN_D)
                cur = srcs[i][rows_b, hs]
                prev = jnp.where(first, halo_s[:, cs], srcs[i][pl.ds(p0, SUBLANES), hs])
                ext = jnp.concatenate([prev, cur], axis=0)
                cw = cw_ref[:, cs]
                y = cur * cw[CONV_K - 1:CONV_K, :]
                for s in range(1, CONV_K):
                    y = y + pltpu.roll(ext, s, 0)[SUBLANES:, :] * cw[CONV_K - 1 - s:CONV_K - s, :]
                return _silu(y)

            conv_silu[b] = conv_silu_b

        k = {ch: l2n(conv_silu[ch[0]](1, ch[1])) for ch in chains}
        k_b = {ch: k[ch].astype(BF16) for ch in chains}
        beta = {(b, h): beta_all[b][:, GDN_HEADS + h:GDN_HEADS + h + 1] for b, h in chains}
        kb = {ch: k[ch] * beta[ch] for ch in chains}
        decay = {}
        for b, h in chains:
            gc = jnp.broadcast_to(gcum[b][:, h:h + 1], (blk, blk))
            gr = jnp.broadcast_to(gcum_t[b][h:h + 1, :], (blk, blk))
            decay[b, h] = jnp.exp(jnp.where(incl, gc - gr, -jnp.inf))
        lower = {ch: jnp.where(strict, _dot_nt(kb[ch].astype(BF16), k_b[ch]) * decay[ch], 0.0)
                 for ch in chains}
        mask = mask_s[0]
        inv = {ch: eye_f - lower[ch] * mask for ch in chains}
        for lvl in range(1, n_levels):
            mask = mask_s[lvl]
            inv_b = {ch: inv[ch].astype(BF16) for ch in chains}
            half = {ch: _mm(inv_b[ch], lower[ch] * mask).astype(BF16) for ch in chains}
            inv = {ch: inv[ch] - _dot(half[ch], inv_b[ch]) for ch in chains}
        inv_b = {ch: inv[ch].astype(BF16) for ch in chains}
        for b, h in chains:
            ch = (b, h)
            eg = e_g[b][:, h:h + 1]
            q = l2n(conv_silu[b](0, h)) * (GDN_D ** -0.5)
            u_s[h, rows[b], :] = _mm(inv_b[ch], conv_silu[b](2, h) * beta[ch])
            w_s[h, rows[b], :] = _mm(inv_b[ch], kb[ch] * eg).astype(BF16)
            qd_s[h, rows[b], :] = (q * eg).astype(BF16)
            kdt_s[h, rows[b], :] = (k[ch] * e_rest[b][:, h:h + 1]).T.astype(BF16)
            at_s[h, rows[b], :] = (_dot_nt(q.astype(BF16), k_b[ch]) * decay[ch]).astype(BF16)
        return carry

    lax.fori_loop(0, nblk // GDN_SOLVE_BLOCKS, solve_phase, 0)

    onorm = onorm_ref[...]

    def scan_phase(c, carry):
        r0 = pl.multiple_of(c * blk, blk)
        rows = pl.ds(r0, blk)
        cd = cd_s[pl.ds(c, 1), :]
        heads = range(GDN_HEADS)
        state = [state_s[h] for h in heads]
        state_b = [state[h].astype(BF16) for h in heads]
        v_new = [u_s[h, rows, :] - _dot(w_s[h, rows, :], state_b[h]) for h in heads]
        v_new_b = [v_new[h].astype(BF16) for h in heads]
        for h in heads:
            state_s[h] = state[h] * cd[:, h:h + 1] + _dot(kdt_s[h, rows, :], v_new_b[h])
        for h in heads:
            hs = slice(h * GDN_D, (h + 1) * GDN_D)
            o = _dot(qd_s[h, rows, :], state_b[h]) + _dot(at_s[h, rows, :], v_new_b[h])
            o = _rms(o, onorm) * _silu(z_ref[rows, hs])
            o_ref[rows, hs] = o.astype(o_ref.dtype)
        return carry

    lax.fori_loop(0, nblk, scan_phase, 0)

    for i in range(3):
        halo_s[:, i * qk:(i + 1) * qk] = srcs[i][tile - SUBLANES:tile, :]


def _gdn(proj3, conv_w, a_log_row, dt_bias_row, out_norm):
    b, s, _ = proj3.shape
    tile = min(GDN_TILE, s)
    qk = GDN_HEADS * GDN_D
    heads_blk = qk // LANES

    def cols(c0):
        return pl.BlockSpec((None, tile, qk), lambda i, t: (i, t, c0 // heads_blk))

    row = lambda width: pl.BlockSpec((1, width), lambda i, t: (0, 0))
    per_head = lambda dtype: pltpu.VMEM((GDN_HEADS, tile, GDN_D), dtype)
    return pl.pallas_call(
        functools.partial(_gdn_kernel, tile=tile),
        grid=(b, s // tile),
        in_specs=[cols(COL_Q), cols(COL_K), cols(COL_V), cols(COL_Z),
                  pl.BlockSpec((None, tile, LANES), lambda i, t: (i, t, COL_SMALL)),
                  pl.BlockSpec((CONV_K, 3 * qk), lambda i, t: (0, 0)),
                  row(LANES), row(LANES), row(GDN_D)],
        out_specs=pl.BlockSpec((None, tile, qk), lambda i, t: (i, t, 0)),
        out_shape=jax.ShapeDtypeStruct((b, s, qk), BF16),
        scratch_shapes=[pltpu.VMEM((GDN_HEADS, GDN_D, GDN_D), F32),
                        pltpu.VMEM((SUBLANES, 3 * qk), F32),
                        pltpu.VMEM((GDN_BLOCK.bit_length() - 1, GDN_BLOCK, GDN_BLOCK), F32),
                        per_head(F32), per_head(BF16), per_head(BF16), per_head(BF16), per_head(BF16),
                        pltpu.VMEM((max(tile // GDN_BLOCK, SUBLANES), LANES), F32)],
        compiler_params=pltpu.CompilerParams(dimension_semantics=("arbitrary", "arbitrary")),
        name="gdn",
    )(proj3, proj3, proj3, proj3, proj3, conv_w, a_log_row, dt_bias_row, out_norm)


def _mla_prep_kernel(cqa_ref, ckv_ref, pos_ref, invf_ref, qn_ref, kvn_ref, wq_ref, wqr_ref, wk_ref, wv_ref,
                     q_ref, k_ref, v_ref):
    cqa = cqa_ref[...]
    c_q = cqa[:, :Q_LORA]
    small = cqa[:, Q_LORA:]
    lane = lax.broadcasted_iota(jnp.int32, small.shape, 1)
    ang = pos_ref[...] * invf_ref[...]
    cos_p = jnp.cos(ang)
    sin_p = jnp.sin(ang)
    unpack = lambda t: jnp.concatenate(
        [pltpu.roll(t, (ROPE_LANE0 - g * MLA_ROPE) % LANES, 1) for g in range(ROPE_PACK)], axis=0)
    cosv = unpack(cos_p)
    sinv = unpack(sin_p)
    half = MLA_ROPE // 2
    lo = (lane >= ROPE_LANE0) & (lane < ROPE_LANE0 + half)
    hi = (lane >= ROPE_LANE0 + half) & (lane < ROPE_LANE0 + MLA_ROPE)
    sin_lo = jnp.where(lo, -sinv, 0.0)
    sin_hi = jnp.where(hi, sinv, 0.0)

    def rotate(t, cos_t):
        return t * cos_t + (pltpu.roll(t, LANES - half, 1) * sin_lo + pltpu.roll(t, half, 1) * sin_hi)

    scale = (MLA_NOPE + MLA_ROPE) ** -0.5 * LOG2_E
    hq = _rms(c_q, qn_ref[...]).astype(BF16)
    q = _dot(hq, wq_ref[...])
    q_swapped = _dot(hq, wqr_ref[...])
    cos_q = jnp.where(lo | hi, cosv, 1.0) * scale
    sin_q = jnp.where(lo | hi, sinv, 0.0) * scale
    hkv = _rms(ckv_ref[...], kvn_ref[...]).astype(BF16)
    k_nope = _dot(hkv, wk_ref[...])
    k_pe = rotate(small, jnp.where(lo | hi, cosv, 0.0))
    for h in range(MLA_HEADS):
        sl = slice(h * LANES, (h + 1) * LANES)
        q_ref[:, sl] = (q[:, sl] * cos_q + q_swapped[:, sl] * sin_q).astype(q_ref.dtype)
        k_ref[:, sl] = (k_nope[:, sl] + k_pe).astype(k_ref.dtype)
    v_ref[...] = _dot(hkv, wv_ref[...]).astype(v_ref.dtype)


def _mla_prep(proj, pos, invf, q_a_norm, kv_a_norm, wq, wq_swapped, wk, wv, *, tm):
    n = proj.shape[0]
    hq = MLA_HEADS * LANES
    hv = MLA_HEADS * MLA_V
    full = lambda shape: pl.BlockSpec(shape, lambda i: (0, 0))
    return pl.pallas_call(
        _mla_prep_kernel,
        grid=(n // tm,),
        in_specs=[pl.BlockSpec((tm, Q_LORA + LANES), lambda i: (i, COL_CQ * LANES // (Q_LORA + LANES))),
                  pl.BlockSpec((tm, KV_LORA), lambda i: (i, COL_CKV * LANES // KV_LORA)),
                  pl.BlockSpec((tm // ROPE_PACK, LANES), lambda i: (i, 0)),
                  full((1, LANES)), full((1, Q_LORA)), full((1, KV_LORA)),
                  full((Q_LORA, hq)), full((Q_LORA, hq)), full((KV_LORA, hq)), full((KV_LORA, hv))],
        out_specs=[pl.BlockSpec((tm, hq), lambda i: (i, 0)),
                   pl.BlockSpec((tm, hq), lambda i: (i, 0)),
                   pl.BlockSpec((tm, hv), lambda i: (i, 0))],
        out_shape=[jax.ShapeDtypeStruct((n, hq), BF16),
                   jax.ShapeDtypeStruct((n, hq), BF16),
                   jax.ShapeDtypeStruct((n, hv), BF16)],
        compiler_params=pltpu.CompilerParams(dimension_semantics=("arbitrary",)),
        name="mla_prep",
    )(proj, proj, pos, invf, q_a_norm, kv_a_norm, wq, wq_swapped, wk, wv)


def _attn_kernel(q_ref, k_ref, v_ref, o_ref, s_s, m_s, l_s, acc_s, *, tq, tk):
    seq = q_ref.shape[0]
    nsub = tq // tk
    lane_tiles = range(tk // LANES)
    r_chunk = lax.broadcasted_iota(jnp.int32, (tq, tk), 0) // CHUNK
    c_chunk = lax.broadcasted_iota(jnp.int32, (tq, tk), 1) // CHUNK
    units = [(qv, h) for qv in range(seq // tq) for h in range(2)]

    def score_steps(unit, slot):
        qv, h = unit
        rows = slice(qv * tq, (qv + 1) * tq)
        head = slice(h * LANES, (h + 1) * LANES)

        def start():
            m_s[slot] = jnp.full((tq, LANES), -jnp.inf, F32)

        def block(j):
            s = _dot_nt(q_ref[rows, head], k_ref[j * tk:(j + 1) * tk, head])
            d = j - qv * nsub
            if d >= 0:
                s = jnp.where(c_chunk + d * (tk // CHUNK) <= r_chunk, s, -jnp.inf)
            s_s[slot, j] = s
            part = m_s[slot]
            for c in lane_tiles:
                part = jnp.maximum(part, s[:, c * LANES:(c + 1) * LANES])
            m_s[slot] = part

        def finish():
            m_s[slot] = jnp.broadcast_to(jnp.max(m_s[slot], axis=-1, keepdims=True), (tq, LANES))
            l_s[slot] = jnp.zeros((tq, LANES), F32)
            acc_s[slot] = jnp.zeros((tq, LANES), F32)

        return [start] + [functools.partial(block, j) for j in range((qv + 1) * nsub)] + [finish]

    def accumulate_steps(unit, slot):
        qv, h = unit
        rows = slice(qv * tq, (qv + 1) * tq)
        out_lanes = slice(h * MLA_V, (h + 1) * MLA_V)

        def block(j):
            m = m_s[slot]
            p = [jnp.exp2(s_s[slot, j, :, c * LANES:(c + 1) * LANES] - m) for c in lane_tiles]
            l_s[slot] += functools.reduce(lambda a, b: a + b, p)
            acc_s[slot] += _dot(jnp.concatenate(p, axis=1).astype(BF16), v_ref[j * tk:(j + 1) * tk, :])

        def finish():
            out = acc_s[slot] * (1.0 / jnp.sum(l_s[slot], axis=-1, keepdims=True))
            o_ref[rows, out_lanes] = out[:, out_lanes].astype(o_ref.dtype)

        return [functools.partial(block, j) for j in range((qv + 1) * nsub)] + [finish]

    pending = []
    for i, unit in enumerate(units):
        current = score_steps(unit, i % 2)
        for k in range(max(len(current), len(pending))):
            if k < len(current):
                current[k]()
            if k < len(pending):
                pending[k]()
        pending = accumulate_steps(unit, i % 2)
    for step in pending:
        step()


def _attention(q3, k3, v3, *, tq, tk):
    b, s, _ = q3.shape
    row_stat = pltpu.VMEM((2, tq, LANES), F32)
    return pl.pallas_call(
        functools.partial(_attn_kernel, tq=tq, tk=tk),
        scratch_shapes=[pltpu.VMEM((2, s // tk, tq, tk), F32), row_stat, row_stat, row_stat],
        grid=(b, MLA_HEADS // 2),
        in_specs=[pl.BlockSpec((None, s, 2 * LANES), lambda i, p: (i, 0, p)),
                  pl.BlockSpec((None, s, 2 * LANES), lambda i, p: (i, 0, p)),
                  pl.BlockSpec((None, s, 2 * MLA_V), lambda i, p: (i, 0, p))],
        out_specs=pl.BlockSpec((None, s, 2 * MLA_V), lambda i, p: (i, 0, p)),
        out_shape=jax.ShapeDtypeStruct((b, s, MLA_HEADS * MLA_V), BF16),
        compiler_params=pltpu.CompilerParams(dimension_semantics=("arbitrary", "arbitrary")),
        name="attention",
    )(q3, k3, v3)


def _merge_kernel(x_ref, oa_ref, ob_ref, gate_ref, gb_ref, wa_ref, wb_ref, wo_ref, o_ref):
    d = x_ref.shape[1]
    g = _sigmoid(gate_ref[...] + gb_ref[...])
    merged = g[:, :d] * _dot(oa_ref[...], wa_ref[...]) + g[:, d:] * _dot(ob_ref[...], wb_ref[...])
    o_ref[...] = x_ref[...] + _dot(merged.astype(BF16), wo_ref[...])


def _merge(x, o_a, o_b, proj, gate_bias, w_a, w_b, w_o, *, tm):
    n, d = x.shape
    full = lambda a: pl.BlockSpec(a.shape, lambda i: (0, 0))
    return pl.pallas_call(
        _merge_kernel,
        grid=(n // tm,),
        in_specs=[pl.BlockSpec((tm, d), lambda i: (i, 0)),
                  pl.BlockSpec((tm, o_a.shape[1]), lambda i: (i, 0)),
                  pl.BlockSpec((tm, o_b.shape[1]), lambda i: (i, 0)),
                  pl.BlockSpec((tm, 2 * d), lambda i: (i, COL_GATE)),
                  full(gate_bias), full(w_a), full(w_b), full(w_o)],
        out_specs=pl.BlockSpec((tm, d), lambda i: (i, 0)),
        out_shape=jax.ShapeDtypeStruct((n, d), F32),
        compiler_params=pltpu.CompilerParams(dimension_semantics=("arbitrary",)),
        name="merge",
    )(x, o_a, o_b, proj, gate_bias, w_a, w_b, w_o)


def _ffn_kernel(*refs, final_norm, sub):
    if final_norm:
        x_ref, g_ref, w1_ref, w3_ref, w2_ref, fin_ref, o_ref, h_s, hid_s, acc_s = refs
    else:
        x_ref, g_ref, w1_ref, w3_ref, w2_ref, o_ref, h_s, hid_s, acc_s = refs
    f = pl.program_id(1)

    @pl.when(f == 0)
    def _():
        x = x_ref[...]
        h_s[...] = _rms(x, g_ref[...]).astype(BF16)
        acc_s[...] = x

    h = h_s[...]
    for c0 in range(0, hid_s.shape[1], sub):
        cols = slice(c0, c0 + sub)
        hid_s[:, cols] = (_silu(_dot(h, w1_ref[:, cols])) * _dot(h, w3_ref[:, cols])).astype(BF16)
    acc_s[...] += _dot(hid_s[...], w2_ref[...])

    @pl.when(f == pl.num_programs(1) - 1)
    def _():
        out = acc_s[...]
        if final_norm:
            out = _rms(out, fin_ref[...])
        o_ref[...] = out


def _ffn(x, gain, w1, w3, w2, *, tm, tf, final_gain=None):
    n, d = x.shape
    dff = w1.shape[1]
    final_norm = final_gain is not None
    row = pl.BlockSpec((1, d), lambda i, f: (0, 0))
    in_specs = [pl.BlockSpec((tm, d), lambda i, f: (i, 0)), row,
                pl.BlockSpec((d, tf), lambda i, f: (0, f)),
                pl.BlockSpec((d, tf), lambda i, f: (0, f)),
                pl.BlockSpec((tf, d), lambda i, f: (f, 0))]
    args = [x, gain, w1, w3, w2]
    if final_norm:
        in_specs.append(row)
        args.append(final_gain)
    return pl.pallas_call(
        functools.partial(_ffn_kernel, final_norm=final_norm, sub=MXU_DIM),
        grid=(n // tm, dff // tf),
        in_specs=in_specs,
        out_specs=pl.BlockSpec((tm, d), lambda i, f: (i, 0)),
        out_shape=jax.ShapeDtypeStruct((n, d), F32),
        scratch_shapes=[pltpu.VMEM((tm, d), BF16), pltpu.VMEM((tm, tf), BF16), pltpu.VMEM((tm, d), F32)],
        compiler_params=pltpu.CompilerParams(dimension_semantics=("arbitrary", "arbitrary"),
                                             vmem_limit_bytes=MOE_VMEM_LIMIT_BYTES),
        name="ffn",
    )(*args)


ROUTE_EXPERT = 0
ROUTE_RANK = 2
ROUTE_WEIGHT = 4


def _router_kernel(x_ref, g_ref, rw_ref, route_ref, count_ref, count_s, tri_s):
    tm = x_ref.shape[0]

    @pl.when(pl.program_id(0) == 0)
    def _():
        count_s[...] = jnp.zeros_like(count_s)
        r = lax.broadcasted_iota(jnp.int32, (tm, tm), 0)
        c = lax.broadcasted_iota(jnp.int32, (tm, tm), 1)
        tri_s[...] = (r > c).astype(F32).astype(BF16)

    lane = lax.broadcasted_iota(jnp.int32, (tm, LANES), 1)
    lane_f = lane.astype(F32)
    h = _rms(x_ref[...], g_ref[...])
    logits = jnp.where(lane < N_EXPERTS, _dot(h, rw_ref[...], HIGHEST), -jnp.inf)
    m1 = jnp.max(logits, axis=-1, keepdims=True)
    i1 = jnp.min(jnp.where(logits == m1, lane_f, float(LANES)), axis=-1, keepdims=True)
    rest = jnp.where(lane_f == i1, -jnp.inf, logits)
    m2 = jnp.max(rest, axis=-1, keepdims=True)
    i2 = jnp.min(jnp.where(rest == m2, lane_f, float(LANES)), axis=-1, keepdims=True)
    t = jnp.exp(m2 - m1)
    sel1 = lane_f == i1
    sel2 = lane_f == i2
    chosen = jnp.where(sel1 | sel2, 1.0, 0.0)
    before = _dot(tri_s[...], chosen.astype(BF16)) + count_s[...]
    r1 = jnp.sum(jnp.where(sel1, before, 0.0), axis=-1, keepdims=True)
    r2 = jnp.sum(jnp.where(sel2, before, 0.0), axis=-1, keepdims=True)
    count_s[...] += jnp.sum(chosen, axis=0, keepdims=True)
    fields = (i1, i2, r1, r2, 1.0 / (1.0 + t), t / (1.0 + t))
    route = jnp.zeros((tm, LANES), F32)
    for k, val in enumerate(fields):
        route = jnp.where(lane == k, val, route)
    route_ref[...] = route
    count_ref[...] = jnp.broadcast_to(count_s[...], count_ref.shape)


def _router(x, gain, router_w, *, tm):
    n, d = x.shape
    return pl.pallas_call(
        _router_kernel,
        grid=(n // tm,),
        in_specs=[pl.BlockSpec((tm, d), lambda i: (i, 0)),
                  pl.BlockSpec((1, d), lambda i: (0, 0)),
                  pl.BlockSpec((d, LANES), lambda i: (0, 0))],
        out_specs=[pl.BlockSpec((tm, LANES), lambda i: (i, 0)),
                   pl.BlockSpec((SUBLANES, LANES), lambda i: (0, 0))],
        out_shape=[jax.ShapeDtypeStruct((n, LANES), F32),
                   jax.ShapeDtypeStruct((SUBLANES, LANES), F32)],
        scratch_shapes=[pltpu.VMEM((1, LANES), F32), pltpu.VMEM((tm, tm), BF16)],
        compiler_params=pltpu.CompilerParams(dimension_semantics=("arbitrary",)),
        name="router",
    )(x, gain, router_w)


def _start_copy(copy):
    copy.start()


def _wait_copy(copy):
    copy.wait()


def _for_each_row_copy(dest_ref, src_at, dst_at, sem, n_rows, action):
    def body(r, carry):
        for s in range(2):
            d = dest_ref[0, 2 * r + s]
            action(pltpu.make_async_copy(src_at(r, s, d), dst_at(r, s, d), sem))
        return carry

    lax.fori_loop(0, n_rows, body, 0, unroll=ROW_COPY_UNROLL)


def _dispatch_kernel(dest_ref, x_ref, grouped_in_ref, grouped_ref, sem):
    del grouped_in_ref
    src_at = lambda r, s, d: x_ref.at[pl.ds(r, 1), :]
    dst_at = lambda r, s, d: grouped_ref.at[pl.ds(d, 1), :]
    _for_each_row_copy(dest_ref, src_at, dst_at, sem, x_ref.shape[0], _start_copy)
    _for_each_row_copy(dest_ref, src_at, dst_at, sem, x_ref.shape[0], _wait_copy)


def _dispatch(x, dest3, grouped_zeros, *, tm):
    n, d = x.shape
    return pl.pallas_call(
        _dispatch_kernel,
        grid=(n // tm,),
        in_specs=[pl.BlockSpec((None, 1, 2 * tm), lambda i: (i, 0, 0), memory_space=pltpu.SMEM),
                  pl.BlockSpec((tm, d), lambda i: (i, 0)),
                  pl.BlockSpec(memory_space=pl.ANY)],
        out_specs=pl.BlockSpec(memory_space=pl.ANY),
        out_shape=jax.ShapeDtypeStruct(grouped_zeros.shape, F32),
        scratch_shapes=[pltpu.SemaphoreType.DMA(())],
        input_output_aliases={2: 0},
        compiler_params=pltpu.CompilerParams(dimension_semantics=("arbitrary",)),
        name="dispatch",
    )(dest3, x, grouped_zeros)


def _experts_kernel(tile_expert_ref, used_ref, x_ref, g_ref, w1_ref, w3_ref, w2_ref, o_ref, *, tf):
    del tile_expert_ref
    i = pl.program_id(0)

    @pl.when(i < used_ref[0])
    def _():
        h = _rms(x_ref[...], g_ref[...]).astype(BF16)
        acc = jnp.zeros(o_ref.shape, F32)
        for f0 in range(0, w1_ref.shape[1], tf):
            hid = _silu(_dot(h, w1_ref[:, f0:f0 + tf])) * _dot(h, w3_ref[:, f0:f0 + tf])
            acc = acc + _dot(hid.astype(BF16), w2_ref[f0:f0 + tf, :])
        o_ref[...] = acc

    @pl.when(i >= used_ref[0])
    def _():
        o_ref[...] = jnp.zeros(o_ref.shape, F32)


def _experts(grouped, gain, w1, w3, w2, tile_expert, used, *, tile, tf):
    p, d = grouped.shape
    dff = w1.shape[2]
    row_tile = lambda i, te, used: (jnp.minimum(i, used[0] - 1), 0)
    expert = lambda i, te, used: (te[i], 0, 0)
    return pl.pallas_call(
        functools.partial(_experts_kernel, tf=tf),
        grid_spec=pltpu.PrefetchScalarGridSpec(
            num_scalar_prefetch=2,
            grid=(p // tile,),
            in_specs=[pl.BlockSpec((tile, d), row_tile),
                      pl.BlockSpec((1, d), lambda i, te, used: (0, 0)),
                      pl.BlockSpec((None, d, dff), expert),
                      pl.BlockSpec((None, d, dff), expert),
                      pl.BlockSpec((None, dff, d), expert)],
            out_specs=pl.BlockSpec((tile, d), lambda i, te, used: (i, 0))),
        out_shape=jax.ShapeDtypeStruct((p, d), F32),
        compiler_params=pltpu.CompilerParams(dimension_semantics=("arbitrary",),
                                             vmem_limit_bytes=MOE_VMEM_LIMIT_BYTES),
        name="experts",
    )(tile_expert, used, grouped, gain, w1, w3, w2)


def _combine_kernel(*refs, final_norm):
    if final_norm:
        dest_ref, x_ref, route_ref, fin_ref, y_ref, o_ref, ya_s, yb_s, sem = refs
    else:
        dest_ref, x_ref, route_ref, y_ref, o_ref, ya_s, yb_s, sem = refs
    bufs = (ya_s, yb_s)
    src_at = lambda r, s, d: y_ref.at[pl.ds(d, 1), :]
    dst_at = lambda r, s, d: bufs[s].at[pl.ds(r, 1), :]
    _for_each_row_copy(dest_ref, src_at, dst_at, sem, x_ref.shape[0], _start_copy)
    _for_each_row_copy(dest_ref, src_at, dst_at, sem, x_ref.shape[0], _wait_copy)
    route = route_ref[...]
    out = (x_ref[...] + route[:, ROUTE_WEIGHT:ROUTE_WEIGHT + 1] * ya_s[...]
           + route[:, ROUTE_WEIGHT + 1:ROUTE_WEIGHT + 2] * yb_s[...])
    if final_norm:
        out = _rms(out, fin_ref[...])
    o_ref[...] = out


def _combine(x, route, dest3, y, *, tm, final_gain=None):
    n, d = x.shape
    final_norm = final_gain is not None
    in_specs = [pl.BlockSpec((None, 1, 2 * tm), lambda i: (i, 0, 0), memory_space=pltpu.SMEM),
                pl.BlockSpec((tm, d), lambda i: (i, 0)),
                pl.BlockSpec((tm, LANES), lambda i: (i, 0))]
    args = [dest3, x, route]
    if final_norm:
        in_specs.append(pl.BlockSpec((1, d), lambda i: (0, 0)))
        args.append(final_gain)
    in_specs.append(pl.BlockSpec(memory_space=pl.ANY))
    args.append(y)
    return pl.pallas_call(
        functools.partial(_combine_kernel, final_norm=final_norm),
        grid=(n // tm,),
        in_specs=in_specs,
        out_specs=pl.BlockSpec((tm, d), lambda i: (i, 0)),
        out_shape=jax.ShapeDtypeStruct((n, d), F32),
        scratch_shapes=[pltpu.VMEM((tm, d), F32), pltpu.VMEM((tm, d), F32), pltpu.SemaphoreType.DMA(())],
        compiler_params=pltpu.CompilerParams(dimension_semantics=("arbitrary",)),
        name="combine",
    )(*args)


def _moe(x, gain, router_w, w1, w3, w2, *, tile, final_gain=None):
    n, d = x.shape
    max_tiles = 2 * n // tile + N_EXPERTS
    route, counts = _router(x, gain, router_w, tm=tile)
    counts = counts[0, :N_EXPERTS].astype(jnp.int32)
    tiles_per = (counts + tile - 1) // tile
    tile_end = jnp.cumsum(tiles_per)
    group_row0 = (tile_end - tiles_per) * tile
    expert = route[:, ROUTE_EXPERT:ROUTE_EXPERT + 2].astype(jnp.int32)
    rank = route[:, ROUTE_RANK:ROUTE_RANK + 2].astype(jnp.int32)
    dest3 = (group_row0[expert] + rank).reshape(n // tile, 1, 2 * tile)
    used = tile_end[-1:]
    tiles = jnp.minimum(jnp.arange(max_tiles, dtype=jnp.int32), used[0] - 1)
    tile_expert = jnp.sum(tiles[:, None] >= tile_end[None, :], axis=1).astype(jnp.int32)
    grouped = _dispatch(x, dest3, jnp.zeros((max_tiles * tile, d), F32), tm=tile)
    y = _experts(grouped, gain, w1, w3, w2, tile_expert, used, tile=tile, tf=w1.shape[2] // 2)
    return _combine(x, route, dest3, y, tm=tile, final_gain=final_gain)


def _pack_positions(pos, tm):
    n = pos.shape[0]
    p = pos.astype(F32).reshape(n // tm, ROPE_PACK, tm // ROPE_PACK).transpose(0, 2, 1)
    return jnp.repeat(p, MLA_ROPE, axis=2).reshape(n // ROPE_PACK, LANES)


def _pad_cols(w, width):
    return jnp.pad(w, ((0, 0), (0, width - w.shape[1])))


def _layout_w_in(w):
    qk = GDN_HEADS * GDN_D
    o = 0
    parts = {}
    for name, width in (("q", qk), ("k", qk), ("v", qk), ("z", qk), ("a", GDN_HEADS), ("b", GDN_HEADS),
                        ("cq", Q_LORA), ("ckv", KV_LORA), ("kr", MLA_ROPE), ("gate", 2 * w.shape[0])):
        parts[name] = w[:, o:o + width]
        o += width
    small = jnp.concatenate(
        [_pad_cols(jnp.concatenate([parts["a"], parts["b"]], axis=1), ROPE_LANE0),
         _pad_cols(parts["kr"], LANES - ROPE_LANE0)], axis=1)
    return jnp.concatenate([parts["gate"], parts["q"], parts["k"], parts["v"], parts["z"], parts["cq"],
                            small, parts["ckv"]], axis=1).astype(BF16)


def _layout_heads(w, widths, pick, pad_to):
    per = sum(widths)
    k = w.shape[0]
    w = w.reshape(k, MLA_HEADS, per)
    start = sum(widths[:pick[0]])
    stop = sum(widths[:pick[1]])
    seg = w[:, :, start:stop]
    seg = jnp.pad(seg, ((0, 0), (0, 0), (0, pad_to - (stop - start))))
    return seg.reshape(k, MLA_HEADS * pad_to).astype(BF16)


def _layout_q_swapped(w):
    k = w.shape[0]
    pe = w.reshape(k, MLA_HEADS, MLA_NOPE + MLA_ROPE)[:, :, MLA_NOPE:]
    half = MLA_ROPE // 2
    seg = jnp.concatenate([-pe[:, :, half:], pe[:, :, :half]], axis=2)
    seg = jnp.pad(seg, ((0, 0), (0, 0), (ROPE_LANE0, LANES - ROPE_LANE0 - MLA_ROPE)))
    return seg.reshape(k, MLA_HEADS * LANES).astype(BF16)


def kernel(x, positions, mix_norm, w_in, gate_bias, conv_w, a_log, dt_bias, gdn_out_norm, w_gdn_o,
           q_a_norm, w_q_b, kv_a_norm, w_kv_b, w_mla_o, w_out, ffn_norm, dense_w1, dense_w3, dense_w2,
           router_w, moe_w1, moe_w3, moe_w2, final_norm):
    b, s, d = x.shape
    n = b * s
    depth = w_in.shape[0]
    xf = x.reshape(n, d)
    pos = _pack_positions(positions.reshape(n), MLA_PREP_TILE)
    inv_freq = 1.0 / (ROPE_THETA ** (jnp.arange(0, MLA_ROPE, 2, dtype=F32) / MLA_ROPE))
    invf = jnp.tile(inv_freq, 2 * ROPE_PACK).reshape(1, LANES)
    row = lambda v: v.reshape(1, -1)

    for l in range(depth):
        proj = _inproj(xf, row(mix_norm[l]), _layout_w_in(w_in[l]), tm=1024, tn=IN_COLS_PAD // 2)
        o_a = _gdn(proj.reshape(b, s, IN_COLS_PAD), conv_w[l], _pad_cols(row(a_log[l]), LANES),
                   _pad_cols(row(dt_bias[l]), LANES), row(gdn_out_norm[l]))
        wq = _layout_heads(w_q_b[l], (MLA_NOPE, MLA_ROPE), (0, 2), LANES)
        wk = _layout_heads(w_kv_b[l], (MLA_NOPE, MLA_V), (0, 1), LANES)
        wv = _layout_heads(w_kv_b[l], (MLA_NOPE, MLA_V), (1, 2), MLA_V)
        q, k, v = _mla_prep(proj, pos, invf, row(q_a_norm[l]), row(kv_a_norm[l]), wq,
                            _layout_q_swapped(w_q_b[l]), wk, wv, tm=MLA_PREP_TILE)
        o_b = _attention(q.reshape(b, s, -1), k.reshape(b, s, -1), v.reshape(b, s, -1), tq=512, tk=512)
        xf = _merge(xf, o_a.reshape(n, -1), o_b.reshape(n, -1), proj, row(gate_bias[l]),
                    w_gdn_o[l].astype(BF16), w_mla_o[l].astype(BF16), w_out[l].astype(BF16), tm=1024)
        fin = row(final_norm) if l == depth - 1 else None
        j = l // 2
        if l % 2 == 0:
            xf = _ffn(xf, row(ffn_norm[l]), dense_w1[j].astype(BF16), dense_w3[j].astype(BF16),
                      dense_w2[j].astype(BF16), tm=1024, tf=1792, final_gain=fin)
        else:
            xf = _moe(xf, row(ffn_norm[l]), _pad_cols(router_w[j], LANES), moe_w1[j].astype(BF16),
                      moe_w3[j].astype(BF16), moe_w2[j].astype(BF16), tile=MOE_TILE, final_gain=fin)
    return xf.reshape(b, s, d)
```

```python
import functools
import math

import jax
import jax.numpy as jnp
from jax import lax
from jax.experimental import pallas as pl
from jax.experimental.pallas import tpu as pltpu

F32 = jnp.float32
BF16 = jnp.bfloat16
HIGHEST = lax.Precision.HIGHEST

LANES = 128
SUBLANES = 8
MXU_DIM = 256
EPS = 1e-6
LOG2_E = math.log2(math.e)
CHUNK = 64
GDN_HEADS = 4
GDN_D = 128
GDN_BLOCK = 128
GDN_TILE = 2048
GDN_SOLVE_BLOCKS = 4
CONV_K = 4
MLA_HEADS = 8
MLA_NOPE = 64
MLA_ROPE = 32
MLA_V = 64
Q_LORA = 384
KV_LORA = 256
ROPE_THETA = 10000.0
N_EXPERTS = 8
MOE_TILE = 512
ROW_COPY_UNROLL = 8
MOE_VMEM_LIMIT_BYTES = 56 * 1024 * 1024

COL_GATE = 0
COL_Q = 16
COL_K = 20
COL_V = 24
COL_Z = 28
COL_CQ = 32
COL_SMALL = 35
COL_CKV = 36
IN_COLS_PAD = 38 * LANES
ROPE_LANE0 = 64
ROPE_PACK = LANES // MLA_ROPE
MLA_PREP_TILE = 512


def _dot(a, b, precision=None):
    return jnp.dot(a, b, preferred_element_type=F32, precision=precision)


def _dot_nt(a, b, precision=None):
    return lax.dot_general(a, b, (((1,), (1,)), ((), ())), preferred_element_type=F32,
                           precision=precision)


def _sigmoid(x):
    return 1.0 / (1.0 + jnp.exp(-x))


def _silu(x):
    return x * _sigmoid(x)


def _rms(x, gain):
    return x * lax.rsqrt(jnp.mean(x * x, axis=-1, keepdims=True) + EPS) * gain


def _mm(a, b, nt=False):
    dot = _dot_nt if nt else _dot
    return dot(a.astype(BF16), b.astype(BF16))


def _inproj_kernel(x_ref, g_ref, w_ref, o_ref):
    h = _rms(x_ref[...], g_ref[...]).astype(BF16)
    o_ref[...] = _dot(h, w_ref[...])


def _inproj(x, gain, w, *, tm, tn):
    n, d = x.shape
    m = w.shape[1]
    return pl.pallas_call(
        _inproj_kernel,
        grid=(m // tn, n // tm),
        in_specs=[pl.BlockSpec((tm, d), lambda j, i: (i, 0)),
                  pl.BlockSpec((1, d), lambda j, i: (0, 0)),
                  pl.BlockSpec((d, tn), lambda j, i: (0, j))],
        out_specs=pl.BlockSpec((tm, tn), lambda j, i: (i, j)),
        out_shape=jax.ShapeDtypeStruct((n, m), F32),
        compiler_params=pltpu.CompilerParams(dimension_semantics=("arbitrary", "arbitrary")),
        name="inproj",
    )(x, gain, w)


def _gdn_kernel(q_ref, k_ref, v_ref, z_ref, sm_ref, cw_ref, alog_ref, dtb_ref, onorm_ref, o_ref,
                state_s, halo_s, mask_s, u_s, w_s, qd_s, kdt_s, at_s, cd_s, *, tile):
    blk = GDN_BLOCK
    nblk = tile // blk
    qk = GDN_HEADS * GDN_D
    n_levels = blk.bit_length() - 1
    row = lax.broadcasted_iota(jnp.int32, (blk, blk), 0)
    col = lax.broadcasted_iota(jnp.int32, (blk, blk), 1)

    @pl.when((pl.program_id(0) == 0) & (pl.program_id(1) == 0))
    def _():
        for lvl in range(n_levels):
            pair = (row >> (lvl + 1)) == (col >> (lvl + 1))
            mask_s[lvl] = (pair & (((row >> lvl) & 1) == 1) & (((col >> lvl) & 1) == 0)).astype(F32)

    @pl.when(pl.program_id(1) == 0)
    def _():
        state_s[...] = jnp.zeros_like(state_s)
        halo_s[...] = jnp.zeros_like(halo_s)

    lane_row = lax.broadcasted_iota(jnp.int32, (1, LANES), 1)
    neg_a = jnp.where(lane_row < GDN_HEADS, -jnp.exp(alog_ref[...]), 0.0)
    dtb = dtb_ref[...]
    srcs = (q_ref, k_ref, v_ref)

    def l2n(x):
        return x * lax.rsqrt(jnp.sum(x * x, axis=-1, keepdims=True) + EPS)

    incl = row >= col
    strict = row > col
    eye_f = (row == col).astype(F32)
    tril = incl.astype(F32).astype(BF16)

    def solve_group(group):
        chains = [(b, h) for b in range(GDN_SOLVE_BLOCKS) for h in range(GDN_HEADS)]
        rows, gcum, gcum_t, e_g, e_rest, beta_all, conv_silu = {}, {}, {}, {}, {}, {}, {}
        for b in range(GDN_SOLVE_BLOCKS):
            c = group * GDN_SOLVE_BLOCKS + b
            r0 = c * blk
            rows[b] = slice(r0, r0 + blk)
            sm = sm_ref[rows[b], :]
            xa = sm + dtb
            g = neg_a * (jnp.maximum(xa, 0.0) + jnp.log(1.0 + jnp.exp(-jnp.abs(xa))))
            beta_all[b] = _sigmoid(sm)
            g_hi = g.astype(BF16)
            g_r = g - g_hi.astype(F32)
            g_mid = g_r.astype(BF16)
            g_lo = (g_r - g_mid.astype(F32)).astype(BF16)
            gcum[b] = _dot(tril, g_hi) + (_dot(tril, g_mid) + _dot(tril, g_lo))
            gcum_t[b] = gcum[b].T
            g_last = gcum[b][blk - 1:blk, :]
            e_g[b] = jnp.exp(gcum[b])
            e_rest[b] = jnp.exp(g_last - gcum[b])
            cd_s[c:c + 1, :] = jnp.exp(g_last)

            def conv_silu_b(i, h, r0=r0, rows_b=rows[b]):
                hs = slice(h * GDN_D, (h + 1) * GDN_D)
                cs = slice(i * qk + h * GDN_D, i * qk + (h + 1) * GDN_D)
                cur = srcs[i][rows_b, hs]
                prev = halo_s[:, cs] if r0 == 0 else srcs[i][r0 - SUBLANES:r0, hs]
                ext = jnp.concatenate([prev, cur], axis=0)
                cw = cw_ref[:, cs]
                y = cur * cw[CONV_K - 1:CONV_K, :]
                for s in range(1, CONV_K):
                    y = y + pltpu.roll(ext, s, 0)[SUBLANES:, :] * cw[CONV_K - 1 - s:CONV_K - s, :]
                return _silu(y)

            conv_silu[b] = conv_silu_b
        yield

        k = {ch: l2n(conv_silu[ch[0]](1, ch[1])) for ch in chains}
        k_b = {ch: k[ch].astype(BF16) for ch in chains}
        beta = {(b, h): beta_all[b][:, GDN_HEADS + h:GDN_HEADS + h + 1] for b, h in chains}
        kb = {ch: k[ch] * beta[ch] for ch in chains}
        yield
        decay = {}
        for b, h in chains:
            gc = jnp.broadcast_to(gcum[b][:, h:h + 1], (blk, blk))
            gr = jnp.broadcast_to(gcum_t[b][h:h + 1, :], (blk, blk))
            decay[b, h] = jnp.exp(jnp.where(incl, gc - gr, -jnp.inf))
        lower = {ch: jnp.where(strict, _dot_nt(kb[ch].astype(BF16), k_b[ch]) * decay[ch], 0.0)
                 for ch in chains}
        mask = mask_s[0]
        inv = {ch: eye_f - lower[ch] * mask for ch in chains}
        for lvl in range(1, n_levels):
            yield
            mask = mask_s[lvl]
            inv_b = {ch: inv[ch].astype(BF16) for ch in chains}
            half = {ch: _mm(inv_b[ch], lower[ch] * mask).astype(BF16) for ch in chains}
            inv = {ch: inv[ch] - _dot(half[ch], inv_b[ch]) for ch in chains}
        yield
        inv_b = {ch: inv[ch].astype(BF16) for ch in chains}
        for b, h in chains:
            ch = (b, h)
            eg = e_g[b][:, h:h + 1]
            q = l2n(conv_silu[b](0, h)) * (GDN_D ** -0.5)
            u_s[h, rows[b], :] = _mm(inv_b[ch], conv_silu[b](2, h) * beta[ch])
            w_s[h, rows[b], :] = _mm(inv_b[ch], kb[ch] * eg).astype(BF16)
            qd_s[h, rows[b], :] = (q * eg).astype(BF16)
            kdt_s[h, rows[b], :] = (k[ch] * e_rest[b][:, h:h + 1]).T.astype(BF16)
            at_s[h, rows[b], :] = (_dot_nt(q.astype(BF16), k_b[ch]) * decay[ch]).astype(BF16)

    onorm = onorm_ref[...]

    def scan_group(group):
        heads = range(GDN_HEADS)
        for c in range(group * GDN_SOLVE_BLOCKS, (group + 1) * GDN_SOLVE_BLOCKS):
            rows = slice(c * blk, (c + 1) * blk)
            cd = cd_s[c:c + 1, :]
            state = [state_s[h] for h in heads]
            state_b = [state[h].astype(BF16) for h in heads]
            v_new = [u_s[h, rows, :] - _dot(w_s[h, rows, :], state_b[h]) for h in heads]
            v_new_b = [v_new[h].astype(BF16) for h in heads]
            for h in heads:
                state_s[h] = state[h] * cd[:, h:h + 1] + _dot(kdt_s[h, rows, :], v_new_b[h])
            yield
            for h in heads:
                hs = slice(h * GDN_D, (h + 1) * GDN_D)
                o = _dot(qd_s[h, rows, :], state_b[h]) + _dot(at_s[h, rows, :], v_new_b[h])
                o = _rms(o, onorm) * _silu(z_ref[rows, hs])
                o_ref[rows, hs] = o.astype(o_ref.dtype)
            yield

    n_groups = nblk // GDN_SOLVE_BLOCKS
    for group in range(n_groups + 1):
        running = []
        if group < n_groups:
            running.append(solve_group(group))
        if group > 0:
            running.append(scan_group(group - 1))
        while running:
            for gen in list(running):
                if next(gen, StopIteration) is StopIteration:
                    running.remove(gen)

    for i in range(3):
        halo_s[:, i * qk:(i + 1) * qk] = srcs[i][tile - SUBLANES:tile, :]


def _gdn(proj3, conv_w, a_log_row, dt_bias_row, out_norm):
    b, s, _ = proj3.shape
    tile = min(GDN_TILE, s)
    qk = GDN_HEADS * GDN_D
    heads_blk = qk // LANES

    def cols(c0):
        return pl.BlockSpec((None, tile, qk), lambda i, t: (i, t, c0 // heads_blk))

    row = lambda width: pl.BlockSpec((1, width), lambda i, t: (0, 0))
    per_head = lambda dtype: pltpu.VMEM((GDN_HEADS, tile, GDN_D), dtype)
    return pl.pallas_call(
        functools.partial(_gdn_kernel, tile=tile),
        grid=(b, s // tile),
        in_specs=[cols(COL_Q), cols(COL_K), cols(COL_V), cols(COL_Z),
                  pl.BlockSpec((None, tile, LANES), lambda i, t: (i, t, COL_SMALL)),
                  pl.BlockSpec((CONV_K, 3 * qk), lambda i, t: (0, 0)),
                  row(LANES), row(LANES), row(GDN_D)],
        out_specs=pl.BlockSpec((None, tile, qk), lambda i, t: (i, t, 0)),
        out_shape=jax.ShapeDtypeStruct((b, s, qk), BF16),
        scratch_shapes=[pltpu.VMEM((GDN_HEADS, GDN_D, GDN_D), F32),
                        pltpu.VMEM((SUBLANES, 3 * qk), F32),
                        pltpu.VMEM((GDN_BLOCK.bit_length() - 1, GDN_BLOCK, GDN_BLOCK), F32),
                        per_head(F32), per_head(BF16), per_head(BF16), per_head(BF16), per_head(BF16),
                        pltpu.VMEM((max(tile // GDN_BLOCK, SUBLANES), LANES), F32)],
        compiler_params=pltpu.CompilerParams(dimension_semantics=("arbitrary", "arbitrary")),
        name="gdn",
    )(proj3, proj3, proj3, proj3, proj3, conv_w, a_log_row, dt_bias_row, out_norm)


def _mla_prep_kernel(cqa_ref, ckv_ref, pos_ref, invf_ref, qn_ref, kvn_ref, wq_ref, wqr_ref, wk_ref, wv_ref,
                     q_ref, k_ref, v_ref):
    cqa = cqa_ref[...]
    c_q = cqa[:, :Q_LORA]
    small = cqa[:, Q_LORA:]
    lane = lax.broadcasted_iota(jnp.int32, small.shape, 1)
    ang = pos_ref[...] * invf_ref[...]
    cos_p = jnp.cos(ang)
    sin_p = jnp.sin(ang)
    unpack = lambda t: jnp.concatenate(
        [pltpu.roll(t, (ROPE_LANE0 - g * MLA_ROPE) % LANES, 1) for g in range(ROPE_PACK)], axis=0)
    cosv = unpack(cos_p)
    sinv = unpack(sin_p)
    half = MLA_ROPE // 2
    lo = (lane >= ROPE_LANE0) & (lane < ROPE_LANE0 + half)
    hi = (lane >= ROPE_LANE0 + half) & (lane < ROPE_LANE0 + MLA_ROPE)
    sin_lo = jnp.where(lo, -sinv, 0.0)
    sin_hi = jnp.where(hi, sinv, 0.0)

    def rotate(t, cos_t):
        return t * cos_t + (pltpu.roll(t, LANES - half, 1) * sin_lo + pltpu.roll(t, half, 1) * sin_hi)

    scale = (MLA_NOPE + MLA_ROPE) ** -0.5 * LOG2_E
    hq = _rms(c_q, qn_ref[...]).astype(BF16)
    q = _dot(hq, wq_ref[...])
    q_swapped = _dot(hq, wqr_ref[...])
    cos_q = jnp.where(lo | hi, cosv, 1.0) * scale
    sin_q = jnp.where(lo | hi, sinv, 0.0) * scale
    hkv = _rms(ckv_ref[...], kvn_ref[...]).astype(BF16)
    k_nope = _dot(hkv, wk_ref[...])
    k_pe = rotate(small, jnp.where(lo | hi, cosv, 0.0))
    for h in range(MLA_HEADS):
        sl = slice(h * LANES, (h + 1) * LANES)
        q_ref[:, sl] = (q[:, sl] * cos_q + q_swapped[:, sl] * sin_q).astype(q_ref.dtype)
        k_ref[:, sl] = (k_nope[:, sl] + k_pe).astype(k_ref.dtype)
    v_ref[...] = _dot(hkv, wv_ref[...]).astype(v_ref.dtype)


def _mla_prep(proj, pos, invf, q_a_norm, kv_a_norm, wq, wq_swapped, wk, wv, *, tm):
    n = proj.shape[0]
    hq = MLA_HEADS * LANES
    hv = MLA_HEADS * MLA_V
    full = lambda shape: pl.BlockSpec(shape, lambda i: (0, 0))
    return pl.pallas_call(
        _mla_prep_kernel,
        grid=(n // tm,),
        in_specs=[pl.BlockSpec((tm, Q_LORA + LANES), lambda i: (i, COL_CQ * LANES // (Q_LORA + LANES))),
                  pl.BlockSpec((tm, KV_LORA), lambda i: (i, COL_CKV * LANES // KV_LORA)),
                  pl.BlockSpec((tm // ROPE_PACK, LANES), lambda i: (i, 0)),
                  full((1, LANES)), full((1, Q_LORA)), full((1, KV_LORA)),
                  full((Q_LORA, hq)), full((Q_LORA, hq)), full((KV_LORA, hq)), full((KV_LORA, hv))],
        out_specs=[pl.BlockSpec((tm, hq), lambda i: (i, 0)),
                   pl.BlockSpec((tm, hq), lambda i: (i, 0)),
                   pl.BlockSpec((tm, hv), lambda i: (i, 0))],
        out_shape=[jax.ShapeDtypeStruct((n, hq), BF16),
                   jax.ShapeDtypeStruct((n, hq), BF16),
                   jax.ShapeDtypeStruct((n, hv), BF16)],
        compiler_params=pltpu.CompilerParams(dimension_semantics=("arbitrary",)),
        name="mla_prep",
    )(proj, proj, pos, invf, q_a_norm, kv_a_norm, wq, wq_swapped, wk, wv)


def _attn_kernel(q_ref, k_ref, v_ref, o_ref, s_s, m_s, l_s, acc_s, *, tq, tk):
    seq = q_ref.shape[0]
    nsub = tq // tk
    lane_tiles = range(tk // LANES)
    r_chunk = lax.broadcasted_iota(jnp.int32, (tq, tk), 0) // CHUNK
    c_chunk = lax.broadcasted_iota(jnp.int32, (tq, tk), 1) // CHUNK
    units = [(qv, h) for qv in range(seq // tq) for h in range(2)]

    def score_steps(unit, slot):
        qv, h = unit
        rows = slice(qv * tq, (qv + 1) * tq)
        head = slice(h * LANES, (h + 1) * LANES)

        def start():
            m_s[slot] = jnp.full((tq, LANES), -jnp.inf, F32)

        def block(j):
            s = _dot_nt(q_ref[rows, head], k_ref[j * tk:(j + 1) * tk, head])
            d = j - qv * nsub
            if d >= 0:
                s = jnp.where(c_chunk + d * (tk // CHUNK) <= r_chunk, s, -jnp.inf)
            s_s[slot, j] = s
            part = m_s[slot]
            for c in lane_tiles:
                part = jnp.maximum(part, s[:, c * LANES:(c + 1) * LANES])
            m_s[slot] = part

        def finish():
            m_s[slot] = jnp.broadcast_to(jnp.max(m_s[slot], axis=-1, keepdims=True), (tq, LANES))
            l_s[slot] = jnp.zeros((tq, LANES), F32)
            acc_s[slot] = jnp.zeros((tq, LANES), F32)

        return [start] + [functools.partial(block, j) for j in range((qv + 1) * nsub)] + [finish]

    def accumulate_steps(unit, slot):
        qv, h = unit
        rows = slice(qv * tq, (qv + 1) * tq)
        out_lanes = slice(h * MLA_V, (h + 1) * MLA_V)

        def block(j):
            m = m_s[slot]
            p = [jnp.exp2(s_s[slot, j, :, c * LANES:(c + 1) * LANES] - m) for c in lane_tiles]
            l_s[slot] += functools.reduce(lambda a, b: a + b, p)
            acc_s[slot] += _dot(jnp.concatenate(p, axis=1).astype(BF16), v_ref[j * tk:(j + 1) * tk, :])

        def finish():
            out = acc_s[slot] * (1.0 / jnp.sum(l_s[slot], axis=-1, keepdims=True))
            o_ref[rows, out_lanes] = out[:, out_lanes].astype(o_ref.dtype)

        return [functools.partial(block, j) for j in range((qv + 1) * nsub)] + [finish]

    pending = []
    for i, unit in enumerate(units):
        current = score_steps(unit, i % 2)
        for k in range(max(len(current), len(pending))):
            if k < len(current):
                current[k]()
            if k < len(pending):
                pending[k]()
        pending = accumulate_steps(unit, i % 2)
    for step in pending:
        step()


def _attention(q3, k3, v3, *, tq, tk):
    b, s, _ = q3.shape
    row_stat = pltpu.VMEM((2, tq, LANES), F32)
    return pl.pallas_call(
        functools.partial(_attn_kernel, tq=tq, tk=tk),
        scratch_shapes=[pltpu.VMEM((2, s // tk, tq, tk), F32), row_stat, row_stat, row_stat],
        grid=(b, MLA_HEADS // 2),
        in_specs=[pl.BlockSpec((None, s, 2 * LANES), lambda i, p: (i, 0, p)),
                  pl.BlockSpec((None, s, 2 * LANES), lambda i, p: (i, 0, p)),
                  pl.BlockSpec((None, s, 2 * MLA_V), lambda i, p: (i, 0, p))],
        out_specs=pl.BlockSpec((None, s, 2 * MLA_V), lambda i, p: (i, 0, p)),
        out_shape=jax.ShapeDtypeStruct((b, s, MLA_HEADS * MLA_V), BF16),
        compiler_params=pltpu.CompilerParams(dimension_semantics=("arbitrary", "arbitrary")),
        name="attention",
    )(q3, k3, v3)


def _merge_kernel(x_ref, oa_ref, ob_ref, gate_ref, gb_ref, wa_ref, wb_ref, wo_ref, o_ref):
    d = x_ref.shape[1]
    g = _sigmoid(gate_ref[...] + gb_ref[...])
    merged = g[:, :d] * _dot(oa_ref[...], wa_ref[...]) + g[:, d:] * _dot(ob_ref[...], wb_ref[...])
    o_ref[...] = x_ref[...] + _dot(merged.astype(BF16), wo_ref[...])


def _merge(x, o_a, o_b, proj, gate_bias, w_a, w_b, w_o, *, tm):
    n, d = x.shape
    full = lambda a: pl.BlockSpec(a.shape, lambda i: (0, 0))
    return pl.pallas_call(
        _merge_kernel,
        grid=(n // tm,),
        in_specs=[pl.BlockSpec((tm, d), lambda i: (i, 0)),
                  pl.BlockSpec((tm, o_a.shape[1]), lambda i: (i, 0)),
                  pl.BlockSpec((tm, o_b.shape[1]), lambda i: (i, 0)),
                  pl.BlockSpec((tm, 2 * d), lambda i: (i, COL_GATE)),
                  full(gate_bias), full(w_a), full(w_b), full(w_o)],
        out_specs=pl.BlockSpec((tm, d), lambda i: (i, 0)),
        out_shape=jax.ShapeDtypeStruct((n, d), F32),
        compiler_params=pltpu.CompilerParams(dimension_semantics=("arbitrary",)),
        name="merge",
    )(x, o_a, o_b, proj, gate_bias, w_a, w_b, w_o)


def _ffn_kernel(*refs, final_norm, sub):
    if final_norm:
        x_ref, g_ref, w1_ref, w3_ref, w2_ref, fin_ref, o_ref, h_s, hid_s, acc_s = refs
    else:
        x_ref, g_ref, w1_ref, w3_ref, w2_ref, o_ref, h_s, hid_s, acc_s = refs
    f = pl.program_id(1)

    @pl.when(f == 0)
    def _():
        x = x_ref[...]
        h_s[...] = _rms(x, g_ref[...]).astype(BF16)
        acc_s[...] = x

    h = h_s[...]
    for c0 in range(0, hid_s.shape[1], sub):
        cols = slice(c0, c0 + sub)
        hid_s[:, cols] = (_silu(_dot(h, w1_ref[:, cols])) * _dot(h, w3_ref[:, cols])).astype(BF16)
    acc_s[...] += _dot(hid_s[...], w2_ref[...])

    @pl.when(f == pl.num_programs(1) - 1)
    def _():
        out = acc_s[...]
        if final_norm:
            out = _rms(out, fin_ref[...])
        o_ref[...] = out


def _ffn(x, gain, w1, w3, w2, *, tm, tf, final_gain=None):
    n, d = x.shape
    dff = w1.shape[1]
    final_norm = final_gain is not None
    row = pl.BlockSpec((1, d), lambda i, f: (0, 0))
    in_specs = [pl.BlockSpec((tm, d), lambda i, f: (i, 0)), row,
                pl.BlockSpec((d, tf), lambda i, f: (0, f)),
                pl.BlockSpec((d, tf), lambda i, f: (0, f)),
                pl.BlockSpec((tf, d), lambda i, f: (f, 0))]
    args = [x, gain, w1, w3, w2]
    if final_norm:
        in_specs.append(row)
        args.append(final_gain)
    return pl.pallas_call(
        functools.partial(_ffn_kernel, final_norm=final_norm, sub=MXU_DIM),
        grid=(n // tm, dff // tf),
        in_specs=in_specs,
        out_specs=pl.BlockSpec((tm, d), lambda i, f: (i, 0)),
        out_shape=jax.ShapeDtypeStruct((n, d), F32),
        scratch_shapes=[pltpu.VMEM((tm, d), BF16), pltpu.VMEM((tm, tf), BF16), pltpu.VMEM((tm, d), F32)],
        compiler_params=pltpu.CompilerParams(dimension_semantics=("arbitrary", "arbitrary"),
                                             vmem_limit_bytes=MOE_VMEM_LIMIT_BYTES),
        name="ffn",
    )(*args)


ROUTE_EXPERT = 0
ROUTE_RANK = 2
ROUTE_WEIGHT = 4


def _router_kernel(x_ref, g_ref, rw_ref, route_ref, count_ref, count_s, tri_s):
    tm = x_ref.shape[0]

    @pl.when(pl.program_id(0) == 0)
    def _():
        count_s[...] = jnp.zeros_like(count_s)
        r = lax.broadcasted_iota(jnp.int32, (tm, tm), 0)
        c = lax.broadcasted_iota(jnp.int32, (tm, tm), 1)
        tri_s[...] = (r > c).astype(F32).astype(BF16)

    lane = lax.broadcasted_iota(jnp.int32, (tm, LANES), 1)
    lane_f = lane.astype(F32)
    h = _rms(x_ref[...], g_ref[...])
    logits = jnp.where(lane < N_EXPERTS, _dot(h, rw_ref[...], HIGHEST), -jnp.inf)
    m1 = jnp.max(logits, axis=-1, keepdims=True)
    i1 = jnp.min(jnp.where(logits == m1, lane_f, float(LANES)), axis=-1, keepdims=True)
    rest = jnp.where(lane_f == i1, -jnp.inf, logits)
    m2 = jnp.max(rest, axis=-1, keepdims=True)
    i2 = jnp.min(jnp.where(rest == m2, lane_f, float(LANES)), axis=-1, keepdims=True)
    t = jnp.exp(m2 - m1)
    sel1 = lane_f == i1
    sel2 = lane_f == i2
    chosen = jnp.where(sel1 | sel2, 1.0, 0.0)
    before = _dot(tri_s[...], chosen.astype(BF16)) + count_s[...]
    r1 = jnp.sum(jnp.where(sel1, before, 0.0), axis=-1, keepdims=True)
    r2 = jnp.sum(jnp.where(sel2, before, 0.0), axis=-1, keepdims=True)
    count_s[...] += jnp.sum(chosen, axis=0, keepdims=True)
    fields = (i1, i2, r1, r2, 1.0 / (1.0 + t), t / (1.0 + t))
    route = jnp.zeros((tm, LANES), F32)
    for k, val in enumerate(fields):
        route = jnp.where(lane == k, val, route)
    route_ref[...] = route
    count_ref[...] = jnp.broadcast_to(count_s[...], count_ref.shape)


def _router(x, gain, router_w, *, tm):
    n, d = x.shape
    return pl.pallas_call(
        _router_kernel,
        grid=(n // tm,),
        in_specs=[pl.BlockSpec((tm, d), lambda i: (i, 0)),
                  pl.BlockSpec((1, d), lambda i: (0, 0)),
                  pl.BlockSpec((d, LANES), lambda i: (0, 0))],
        out_specs=[pl.BlockSpec((tm, LANES), lambda i: (i, 0)),
                   pl.BlockSpec((SUBLANES, LANES), lambda i: (0, 0))],
        out_shape=[jax.ShapeDtypeStruct((n, LANES), F32),
                   jax.ShapeDtypeStruct((SUBLANES, LANES), F32)],
        scratch_shapes=[pltpu.VMEM((1, LANES), F32), pltpu.VMEM((tm, tm), BF16)],
        compiler_params=pltpu.CompilerParams(dimension_semantics=("arbitrary",)),
        name="router",
    )(x, gain, router_w)


def _start_copy(copy):
    copy.start()


def _wait_copy(copy):
    copy.wait()


def _for_each_row_copy(dest_ref, src_at, dst_at, sem, n_rows, action):
    def body(r, carry):
        for s in range(2):
            d = dest_ref[0, 2 * r + s]
            action(pltpu.make_async_copy(src_at(r, s, d), dst_at(r, s, d), sem))
        return carry

    lax.fori_loop(0, n_rows, body, 0, unroll=ROW_COPY_UNROLL)


def _dispatch_kernel(dest_ref, x_ref, grouped_in_ref, grouped_ref, sem):
    del grouped_in_ref
    src_at = lambda r, s, d: x_ref.at[pl.ds(r, 1), :]
    dst_at = lambda r, s, d: grouped_ref.at[pl.ds(d, 1), :]
    _for_each_row_copy(dest_ref, src_at, dst_at, sem, x_ref.shape[0], _start_copy)
    _for_each_row_copy(dest_ref, src_at, dst_at, sem, x_ref.shape[0], _wait_copy)


def _dispatch(x, dest3, grouped_zeros, *, tm):
    n, d = x.shape
    return pl.pallas_call(
        _dispatch_kernel,
        grid=(n // tm,),
        in_specs=[pl.BlockSpec((None, 1, 2 * tm), lambda i: (i, 0, 0), memory_space=pltpu.SMEM),
                  pl.BlockSpec((tm, d), lambda i: (i, 0)),
                  pl.BlockSpec(memory_space=pl.ANY)],
        out_specs=pl.BlockSpec(memory_space=pl.ANY),
        out_shape=jax.ShapeDtypeStruct(grouped_zeros.shape, F32),
        scratch_shapes=[pltpu.SemaphoreType.DMA(())],
        input_output_aliases={2: 0},
        compiler_params=pltpu.CompilerParams(dimension_semantics=("arbitrary",)),
        name="dispatch",
    )(dest3, x, grouped_zeros)


def _experts_kernel(tile_expert_ref, used_ref, x_ref, g_ref, w1_ref, w3_ref, w2_ref, o_ref, *, tf):
    del tile_expert_ref
    i = pl.program_id(0)

    @pl.when(i < used_ref[0])
    def _():
        h = _rms(x_ref[...], g_ref[...]).astype(BF16)
        acc = jnp.zeros(o_ref.shape, F32)
        for f0 in range(0, w1_ref.shape[1], tf):
            hid = _silu(_dot(h, w1_ref[:, f0:f0 + tf])) * _dot(h, w3_ref[:, f0:f0 + tf])
            acc = acc + _dot(hid.astype(BF16), w2_ref[f0:f0 + tf, :])
        o_ref[...] = acc

    @pl.when(i >= used_ref[0])
    def _():
        o_ref[...] = jnp.zeros(o_ref.shape, F32)


def _experts(grouped, gain, w1, w3, w2, tile_expert, used, *, tile, tf):
    p, d = grouped.shape
    dff = w1.shape[2]
    row_tile = lambda i, te, used: (jnp.minimum(i, used[0] - 1), 0)
    expert = lambda i, te, used: (te[i], 0, 0)
    return pl.pallas_call(
        functools.partial(_experts_kernel, tf=tf),
        grid_spec=pltpu.PrefetchScalarGridSpec(
            num_scalar_prefetch=2,
            grid=(p // tile,),
            in_specs=[pl.BlockSpec((tile, d), row_tile),
                      pl.BlockSpec((1, d), lambda i, te, used: (0, 0)),
                      pl.BlockSpec((None, d, dff), expert),
                      pl.BlockSpec((None, d, dff), expert),
                      pl.BlockSpec((None, dff, d), expert)],
            out_specs=pl.BlockSpec((tile, d), lambda i, te, used: (i, 0))),
        out_shape=jax.ShapeDtypeStruct((p, d), F32),
        compiler_params=pltpu.CompilerParams(dimension_semantics=("arbitrary",),
                                             vmem_limit_bytes=MOE_VMEM_LIMIT_BYTES),
        name="experts",
    )(tile_expert, used, grouped, gain, w1, w3, w2)


def _combine_kernel(*refs, final_norm):
    if final_norm:
        dest_ref, x_ref, route_ref, fin_ref, y_ref, o_ref, ya_s, yb_s, sem = refs
    else:
        dest_ref, x_ref, route_ref, y_ref, o_ref, ya_s, yb_s, sem = refs
    bufs = (ya_s, yb_s)
    src_at = lambda r, s, d: y_ref.at[pl.ds(d, 1), :]
    dst_at = lambda r, s, d: bufs[s].at[pl.ds(r, 1), :]
    _for_each_row_copy(dest_ref, src_at, dst_at, sem, x_ref.shape[0], _start_copy)
    _for_each_row_copy(dest_ref, src_at, dst_at, sem, x_ref.shape[0], _wait_copy)
    route = route_ref[...]
    out = (x_ref[...] + route[:, ROUTE_WEIGHT:ROUTE_WEIGHT + 1] * ya_s[...]
           + route[:, ROUTE_WEIGHT + 1:ROUTE_WEIGHT + 2] * yb_s[...])
    if final_norm:
        out = _rms(out, fin_ref[...])
    o_ref[...] = out


def _combine(x, route, dest3, y, *, tm, final_gain=None):
    n, d = x.shape
    final_norm = final_gain is not None
    in_specs = [pl.BlockSpec((None, 1, 2 * tm), lambda i: (i, 0, 0), memory_space=pltpu.SMEM),
                pl.BlockSpec((tm, d), lambda i: (i, 0)),
                pl.BlockSpec((tm, LANES), lambda i: (i, 0))]
    args = [dest3, x, route]
    if final_norm:
        in_specs.append(pl.BlockSpec((1, d), lambda i: (0, 0)))
        args.append(final_gain)
    in_specs.append(pl.BlockSpec(memory_space=pl.ANY))
    args.append(y)
    return pl.pallas_call(
        functools.partial(_combine_kernel, final_norm=final_norm),
        grid=(n // tm,),
        in_specs=in_specs,
        out_specs=pl.BlockSpec((tm, d), lambda i: (i, 0)),
        out_shape=jax.ShapeDtypeStruct((n, d), F32),
        scratch_shapes=[pltpu.VMEM((tm, d), F32), pltpu.VMEM((tm, d), F32), pltpu.SemaphoreType.DMA(())],
        compiler_params=pltpu.CompilerParams(dimension_semantics=("arbitrary",)),
        name="combine",
    )(*args)


def _moe(x, gain, router_w, w1, w3, w2, *, tile, final_gain=None):
    n, d = x.shape
    max_tiles = 2 * n // tile + N_EXPERTS
    route, counts = _router(x, gain, router_w, tm=tile)
    counts = counts[0, :N_EXPERTS].astype(jnp.int32)
    tiles_per = (counts + tile - 1) // tile
    tile_end = jnp.cumsum(tiles_per)
    group_row0 = (tile_end - tiles_per) * tile
    expert = route[:, ROUTE_EXPERT:ROUTE_EXPERT + 2].astype(jnp.int32)
    rank = route[:, ROUTE_RANK:ROUTE_RANK + 2].astype(jnp.int32)
    dest3 = (group_row0[expert] + rank).reshape(n // tile, 1, 2 * tile)
    used = tile_end[-1:]
    tiles = jnp.minimum(jnp.arange(max_tiles, dtype=jnp.int32), used[0] - 1)
    tile_expert = jnp.sum(tiles[:, None] >= tile_end[None, :], axis=1).astype(jnp.int32)
    grouped = _dispatch(x, dest3, jnp.zeros((max_tiles * tile, d), F32), tm=tile)
    y = _experts(grouped, gain, w1, w3, w2, tile_expert, used, tile=tile, tf=w1.shape[2] // 2)
    return _combine(x, route, dest3, y, tm=tile, final_gain=final_gain)


def _pack_positions(pos, tm):
    n = pos.shape[0]
    p = pos.astype(F32).reshape(n // tm, ROPE_PACK, tm // ROPE_PACK).transpose(0, 2, 1)
    return jnp.repeat(p, MLA_ROPE, axis=2).reshape(n // ROPE_PACK, LANES)


def _pad_cols(w, width):
    return jnp.pad(w, ((0, 0), (0, width - w.shape[1])))


def _layout_w_in(w):
    qk = GDN_HEADS * GDN_D
    o = 0
    parts = {}
    for name, width in (("q", qk), ("k", qk), ("v", qk), ("z", qk), ("a", GDN_HEADS), ("b", GDN_HEADS),
                        ("cq", Q_LORA), ("ckv", KV_LORA), ("kr", MLA_ROPE), ("gate", 2 * w.shape[0])):
        parts[name] = w[:, o:o + width]
        o += width
    small = jnp.concatenate(
        [_pad_cols(jnp.concatenate([parts["a"], parts["b"]], axis=1), ROPE_LANE0),
         _pad_cols(parts["kr"], LANES - ROPE_LANE0)], axis=1)
    return jnp.concatenate([parts["gate"], parts["q"], parts["k"], parts["v"], parts["z"], parts["cq"],
                            small, parts["ckv"]], axis=1).astype(BF16)


def _layout_heads(w, widths, pick, pad_to):
    per = sum(widths)
    k = w.shape[0]
    w = w.reshape(k, MLA_HEADS, per)
    start = sum(widths[:pick[0]])
    stop = sum(widths[:pick[1]])
    seg = w[:, :, start:stop]
    seg = jnp.pad(seg, ((0, 0), (0, 0), (0, pad_to - (stop - start))))
    return seg.reshape(k, MLA_HEADS * pad_to).astype(BF16)


def _layout_q_swapped(w):
    k = w.shape[0]
    pe = w.reshape(k, MLA_HEADS, MLA_NOPE + MLA_ROPE)[:, :, MLA_NOPE:]
    half = MLA_ROPE // 2
    seg = jnp.concatenate([-pe[:, :, half:], pe[:, :, :half]], axis=2)
    seg = jnp.pad(seg, ((0, 0), (0, 0), (ROPE_LANE0, LANES - ROPE_LANE0 - MLA_ROPE)))
    return seg.reshape(k, MLA_HEADS * LANES).astype(BF16)


def kernel(x, positions, mix_norm, w_in, gate_bias, conv_w, a_log, dt_bias, gdn_out_norm, w_gdn_o,
           q_a_norm, w_q_b, kv_a_norm, w_kv_b, w_mla_o, w_out, ffn_norm, dense_w1, dense_w3, dense_w2,
           router_w, moe_w1, moe_w3, moe_w2, final_norm):
    b, s, d = x.shape
    n = b * s
    depth = w_in.shape[0]
    xf = x.reshape(n, d)
    pos = _pack_positions(positions.reshape(n), MLA_PREP_TILE)
    inv_freq = 1.0 / (ROPE_THETA ** (jnp.arange(0, MLA_ROPE, 2, dtype=F32) / MLA_ROPE))
    invf = jnp.tile(inv_freq, 2 * ROPE_PACK).reshape(1, LANES)
    row = lambda v: v.reshape(1, -1)

    for l in range(depth):
        proj = _inproj(xf, row(mix_norm[l]), _layout_w_in(w_in[l]), tm=1024, tn=IN_COLS_PAD // 2)
        o_a = _gdn(proj.reshape(b, s, IN_COLS_PAD), conv_w[l], _pad_cols(row(a_log[l]), LANES),
                   _pad_cols(row(dt_bias[l]), LANES), row(gdn_out_norm[l]))
        wq = _layout_heads(w_q_b[l], (MLA_NOPE, MLA_ROPE), (0, 2), LANES)
        wk = _layout_heads(w_kv_b[l], (MLA_NOPE, MLA_V), (0, 1), LANES)
        wv = _layout_heads(w_kv_b[l], (MLA_NOPE, MLA_V), (1, 2), MLA_V)
        q, k, v = _mla_prep(proj, pos, invf, row(q_a_norm[l]), row(kv_a_norm[l]), wq,
                            _layout_q_swapped(w_q_b[l]), wk, wv, tm=MLA_PREP_TILE)
        o_b = _attention(q.reshape(b, s, -1), k.reshape(b, s, -1), v.reshape(b, s, -1), tq=512, tk=512)
        xf = _merge(xf, o_a.reshape(n, -1), o_b.reshape(n, -1), proj, row(gate_bias[l]),
                    w_gdn_o[l].astype(BF16), w_mla_o[l].astype(BF16), w_out[l].astype(BF16), tm=1024)
        fin = row(final_norm) if l == depth - 1 else None
        j = l // 2
        if l % 2 == 0:
            xf = _ffn(xf, row(ffn_norm[l]), dense_w1[j].astype(BF16), dense_w3[j].astype(BF16),
                      dense_w2[j].astype(BF16), tm=1024, tf=1792, final_gain=fin)
        else:
            xf = _moe(xf, row(ffn_norm[l]), _pad_cols(router_w[j], LANES), moe_w1[j].astype(BF16),
                      moe_w3[j].astype(BF16), moe_w2[j].astype(BF16), tile=MOE_TILE, final_gain=fin)
    return xf.reshape(b, s, d)
```
